```python
import jax, jax.numpy as jnp
from jax import lax
import numpy as np

D_MODEL = 2048
BATCH = 4
SEQ = 4096
DEPTH = 1
DEC_BATCH = 2
DEC_SEQ = 8192
PAST_LEN = 128

D_MIX = D_MODEL
C_CONV = D_MIX // 2
CONV_WIDTH = 31
HEAD_DIM = 128
N_HEADS = (D_MIX - C_CONV) // HEAD_DIM
N_KV_HEADS = 2
GROUP = N_HEADS // N_KV_HEADS
WINDOW = 128
BLOCK = 128
ROPE_THETA = 10000.0
NORM_EPS = 1e-6
LN_EPS = 1e-5
D_ATTN = N_HEADS * HEAD_DIM
D_KV = N_KV_HEADS * HEAD_DIM
SPLITS = (C_CONV, C_CONV, C_CONV, D_ATTN, D_KV, D_KV, D_ATTN)
D_IN_PROJ = 3 * C_CONV + 2 * D_ATTN + 2 * D_KV

kernel_name = "hymba_conformer_swa_encoder"


def rms_norm(x, g):
    xf = x.astype(jnp.float32)
    y = xf * lax.rsqrt(jnp.mean(xf * xf, axis=-1, keepdims=True) + NORM_EPS)
    return (y * g.astype(jnp.float32)).astype(x.dtype)


def layer_norm(x, g, b):
    xf = x.astype(jnp.float32)
    mu = jnp.mean(xf, axis=-1, keepdims=True)
    xc = xf - mu
    var = jnp.mean(xc * xc, axis=-1, keepdims=True)
    y = xc * lax.rsqrt(var + LN_EPS) * g.astype(jnp.float32) + b.astype(jnp.float32)
    return y.astype(x.dtype)


def apply_rope(t, pos):
    half = HEAD_DIM // 2
    inv_freq = 1.0 / (ROPE_THETA ** (jnp.arange(half, dtype=jnp.float32) / half))
    ang = pos.astype(jnp.float32)[:, None] * inv_freq[None, :]
    cos = jnp.cos(ang)[None, :, None, :]
    sin = jnp.sin(ang)[None, :, None, :]
    tf = t.astype(jnp.float32)
    t1, t2 = tf[..., :half], tf[..., half:]
    out = jnp.concatenate([t1 * cos - t2 * sin, t2 * cos + t1 * sin], axis=-1)
    return out.astype(t.dtype)


def conformer_conv(u_val, u_glu, w_dw, b_dw, ln_g, ln_b, w_pw):
    h = u_val * jax.nn.sigmoid(u_glu)
    pad = CONV_WIDTH // 2
    h = lax.conv_general_dilated(
        h, w_dw[:, None, :].astype(h.dtype),
        window_strides=(1,), padding=[(pad, pad)],
        dimension_numbers=("NWC", "WIO", "NWC"),
        feature_group_count=C_CONV) + b_dw
    h = jax.nn.silu(layer_norm(h, ln_g, ln_b))
    return h @ w_pw


def banded_sink_attention(q, k, v, sink):
    B, S = q.shape[0], q.shape[1]
    nb = S // BLOCK
    qb = q.reshape(B, nb, BLOCK, N_KV_HEADS, GROUP, HEAD_DIM)

    def band(t):
        tp = jnp.pad(t, ((0, 0), (BLOCK, BLOCK), (0, 0), (0, 0)))
        tp = tp.reshape(B, nb + 2, BLOCK, N_KV_HEADS, HEAD_DIM)
        return jnp.concatenate([tp[:, :-2], tp[:, 1:-1], tp[:, 2:]], axis=2)

    kb, vb = band(k), band(v)
    scale = HEAD_DIM ** -0.5
    s = jnp.einsum("bnqkgd,bnjkd->bnkgqj", qb, kb).astype(jnp.float32) * scale
    blk = jnp.arange(nb)[:, None]
    qpos = blk * BLOCK + jnp.arange(BLOCK)[None, :]
    kpos = (blk - 1) * BLOCK + jnp.arange(3 * BLOCK)[None, :]
    rel = kpos[:, None, :] - qpos[:, :, None]
    valid = (jnp.abs(rel) <= WINDOW) & (kpos[:, None, :] >= 0) & (kpos[:, None, :] < S)
    s = jnp.where(valid[None, :, None, None], s, -1e30)
    sink_f = sink.astype(jnp.float32).reshape(1, 1, N_KV_HEADS, GROUP, 1, 1)
    m = jnp.maximum(jnp.max(s, axis=-1, keepdims=True), sink_f)
    p = jnp.exp(s - m)
    denom = jnp.sum(p, axis=-1, keepdims=True) + jnp.exp(sink_f - m)
    probs = (p / denom).astype(v.dtype)
    o = jnp.einsum("bnkgqj,bnjkd->bnqkgd", probs, vb)
    return o.reshape(B, S, D_ATTN)


def mixer_layer(x, norm_g, w_in, w_dw, b_dw, ln_g, ln_b, w_pw, sink, w_out):
    B, S, _ = x.shape
    h = rms_norm(x, norm_g)
    z = h @ w_in
    split_points = [int(i) for i in np.cumsum(SPLITS)[:-1]]
    c_val, c_glu, c_gate, q, k, v, a_gate = jnp.split(z, split_points, axis=-1)
    pos = jnp.arange(S)
    q = apply_rope(q.reshape(B, S, N_HEADS, HEAD_DIM), pos)
    k = apply_rope(k.reshape(B, S, N_KV_HEADS, HEAD_DIM), pos)
    v = v.reshape(B, S, N_KV_HEADS, HEAD_DIM)
    conv_out = conformer_conv(c_val, c_glu, w_dw, b_dw, ln_g, ln_b, w_pw) * jax.nn.silu(c_gate)
    attn_out = banded_sink_attention(q, k, v, sink) * jax.nn.silu(a_gate)
    mix = jnp.concatenate([conv_out, attn_out], axis=-1) @ w_out
    return x + mix


def encoder(x, norm_g, w_in, w_dw, b_dw, conv_ln_g, conv_ln_b, w_pw, attn_sink, w_out, final_norm_g):
    for l in range(DEPTH):
        x = mixer_layer(x, norm_g[l], w_in[l], w_dw[l], b_dw[l], conv_ln_g[l], conv_ln_b[l],
                        w_pw[l], attn_sink[l], w_out[l])
    return rms_norm(x, final_norm_g)


def setup_inputs(seed: int = 0) -> dict:
    key = jax.random.key(seed)
    ks = jax.random.split(key, 14)
    f32 = jnp.float32
    return {
        "x_prompt": jax.random.normal(ks[0], (BATCH, SEQ, D_MODEL), f32),
        "x_sample": jax.random.normal(ks[1], (DEC_BATCH, DEC_SEQ, D_MODEL), f32),
        "norm_g": 1.0 + 0.02 * jax.random.normal(ks[2], (DEPTH, D_MODEL), f32),
        "w_in": jax.random.normal(ks[3], (DEPTH, D_MODEL, D_IN_PROJ), f32) * D_MODEL ** -0.5,
        "w_dw": jax.random.normal(ks[4], (DEPTH, CONV_WIDTH, C_CONV), f32) * CONV_WIDTH ** -0.5,
        "b_dw": 0.02 * jax.random.normal(ks[5], (DEPTH, C_CONV), f32),
        "conv_ln_g": 1.0 + 0.02 * jax.random.normal(ks[6], (DEPTH, C_CONV), f32),
        "conv_ln_b": 0.02 * jax.random.normal(ks[7], (DEPTH, C_CONV), f32),
        "w_pw": jax.random.normal(ks[8], (DEPTH, C_CONV, C_CONV), f32) * C_CONV ** -0.5,
        "attn_sink": jax.random.normal(ks[9], (DEPTH, N_HEADS), f32),
        "w_out": jax.random.normal(ks[10], (DEPTH, D_MIX, D_MODEL), f32) * D_MIX ** -0.5,
        "final_norm_g": 1.0 + 0.02 * jax.random.normal(ks[11], (D_MODEL,), f32),
    }


def reference(x_prompt, x_sample, norm_g, w_in, w_dw, b_dw, conv_ln_g, conv_ln_b, w_pw,
              attn_sink, w_out, final_norm_g):
    y_prompt = encoder(x_prompt, norm_g, w_in, w_dw, b_dw, conv_ln_g, conv_ln_b, w_pw,
                       attn_sink, w_out, final_norm_g)
    y_sample = encoder(x_sample, norm_g, w_in, w_dw, b_dw, conv_ln_g, conv_ln_b, w_pw,
                       attn_sink, w_out, final_norm_g)
    return (y_prompt, y_sample)
```

```python
import functools

import jax
import jax.numpy as jnp
from jax import lax
from jax.experimental import pallas as pl
from jax.experimental.pallas import tpu as pltpu

D_MODEL = 2048
C_CONV = 1024
CONV_WIDTH = 31
CONV_PAD = CONV_WIDTH // 2
HEAD_DIM = 128
N_HEADS = 8
N_KV_HEADS = 2
GROUP = N_HEADS // N_KV_HEADS
BLOCK = 128
ROPE_THETA = 10000.0
NORM_EPS = 1e-6
LN_EPS = 1e-5
D_ATTN = N_HEADS * HEAD_DIM
D_KV = N_KV_HEADS * HEAD_DIM
D_ZF = 3 * C_CONV + D_ATTN
D_ZB = D_ATTN + 2 * D_KV
MASK_VALUE = -1e30

F32 = jnp.float32
BF16 = jnp.bfloat16

LANES = 128
HALO_ROWS = 16
TM_PROJ = 1024
TN_PROJ = 512
T_CONV = 512
T_ATTN = 512
TM_OUT = 512
VMEM_LIMIT = 56 * 1024 * 1024

_ARB = pltpu.ARBITRARY


def _silu(x):
    return x * jax.nn.sigmoid(x)


def _inproj_kernel(x_ref, g_ref, w_ref, cq_ref, sq_ref, ck_ref, sk_ref, zf_ref, zb_ref, h_ref, *, tm):
    j = pl.program_id(1)
    nf = D_ZF // TN_PROJ
    rc = 64

    @pl.when(j == 0)
    def _norm():
        def body(r, carry):
            rows = pl.ds(pl.multiple_of(r * rc, rc), rc)
            xx = x_ref[rows, :]
            ms = jnp.mean(xx * xx, axis=-1, keepdims=True)
            h_ref[rows, :] = (xx * lax.rsqrt(ms + NORM_EPS) * g_ref[...]).astype(BF16)
            return carry
        lax.fori_loop(0, tm // rc, body, 0)

    def zdot():
        return jnp.dot(h_ref[...], w_ref[...], preferred_element_type=F32)

    def rope(t, c_ref, s_ref):
        return t * c_ref[...] + pltpu.roll(t, HEAD_DIM // 2, 1) * s_ref[...]

    @pl.when(j < nf)
    def _f32():
        zf_ref[...] = zdot()

    @pl.when((j >= nf) & (j < nf + D_ATTN // TN_PROJ))
    def _q():
        z = zdot()
        for hh in range(TN_PROJ // HEAD_DIM):
            cols = slice(hh * HEAD_DIM, (hh + 1) * HEAD_DIM)
            zb_ref[:, cols] = rope(z[:, cols], cq_ref, sq_ref).astype(BF16)

    @pl.when(j == nf + D_ATTN // TN_PROJ)
    def _kv():
        z = zdot()
        for hh in range(N_KV_HEADS):
            cols = slice(hh * HEAD_DIM, (hh + 1) * HEAD_DIM)
            zb_ref[:, cols] = rope(z[:, cols], ck_ref, sk_ref).astype(BF16)
        zb_ref[:, D_KV:2 * D_KV] = z[:, D_KV:2 * D_KV].astype(BF16)


def _inproj(x2, g, w, tabs, seq):
    m = x2.shape[0]
    tm = min(TM_PROJ, seq)
    nf = D_ZF // TN_PROJ
    nj = (D_ZF + D_ZB) // TN_PROJ
    tiles_per_seq = seq // tm
    tab_spec = pl.BlockSpec((tm, HEAD_DIM), lambda i, j: (i % tiles_per_seq, 0))
    return pl.pallas_call(
        functools.partial(_inproj_kernel, tm=tm),
        out_shape=(jax.ShapeDtypeStruct((m, D_ZF), F32), jax.ShapeDtypeStruct((m, D_ZB), BF16)),
        grid=(m // tm, nj),
        in_specs=[
            pl.BlockSpec((tm, D_MODEL), lambda i, j: (i, 0)),
            pl.BlockSpec((1, D_MODEL), lambda i, j: (0, 0)),
            pl.BlockSpec((D_MODEL, TN_PROJ), lambda i, j: (0, j)),
            tab_spec, tab_spec, tab_spec, tab_spec,
        ],
        out_specs=(
            pl.BlockSpec((tm, TN_PROJ), lambda i, j: (i, jnp.minimum(j, nf - 1))),
            pl.BlockSpec((tm, TN_PROJ), lambda i, j: (i, jnp.maximum(j - nf, 0))),
        ),
        scratch_shapes=[pltpu.VMEM((tm, D_MODEL), BF16)],
        compiler_params=pltpu.CompilerParams(
            dimension_semantics=(_ARB, _ARB), vmem_limit_bytes=VMEM_LIMIT),
        name="inproj",
    )(x2, g, w, *tabs)


def _conv_kernel(val_ref, glu_ref, gate_ref, valp_ref, glup_ref, valn_ref, glun_ref,
                 wdw_ref, bdw_ref, lng_ref, lnb_ref, wpw_ref, out_ref,
                 hbuf, ybuf, abuf, *, t, tiles_per_seq):
    i = pl.program_id(0)
    t_in_seq = i % tiles_per_seq
    at_start = t_in_seq == 0
    at_end = t_in_seq == tiles_per_seq - 1

    ncb = C_CONV // LANES
    hp = jnp.where(at_start, 0.0, valp_ref[...] * jax.nn.sigmoid(glup_ref[...]))
    hn = jnp.where(at_end, 0.0, valn_ref[...] * jax.nn.sigmoid(glun_ref[...]))
    for cb in range(ncb):
        lanes = slice(cb * LANES, (cb + 1) * LANES)
        hbuf[cb, 0:HALO_ROWS, :] = hp[:, lanes]
        hbuf[cb, t + HALO_ROWS:t + 2 * HALO_ROWS, :] = hn[:, lanes]

    ra = 32

    def glu_body(r, carry):
        r0 = pl.multiple_of(r * ra, ra)
        rows = pl.ds(r0, ra)
        h = val_ref[rows, :] * jax.nn.sigmoid(glu_ref[rows, :])
        for cb in range(ncb):
            hbuf[cb, pl.ds(r0 + HALO_ROWS, ra), :] = h[:, cb * LANES:(cb + 1) * LANES]
        return carry
    lax.fori_loop(0, t // ra, glu_body, 0)

    rb = 64
    shift = HALO_ROWS - CONV_PAD

    def conv_body(r, carry):
        r0 = pl.multiple_of(r * (2 * rb), 2 * rb)
        for cb in range(ncb):
            lanes = slice(cb * LANES, (cb + 1) * LANES)
            acc_e = jnp.broadcast_to(bdw_ref[:, lanes], (rb, LANES))
            acc_o = acc_e
            for k in range(CONV_WIDTH):
                w = wdw_ref[k:k + 1, lanes]
                acc_e = acc_e + w * hbuf[cb, pl.ds(r0 + (k + shift), rb, stride=2), :]
                acc_o = acc_o + w * hbuf[cb, pl.ds(r0 + (k + shift + 1), rb, stride=2), :]
            ybuf[cb, pl.ds(r0, rb, stride=2), :] = acc_e
            ybuf[cb, pl.ds(r0 + 1, rb, stride=2), :] = acc_o
        return carry
    lax.fori_loop(0, t // (2 * rb), conv_body, 0)

    rl = 32

    def ln_body(r, carry):
        rows = pl.ds(pl.multiple_of(r * rl, rl), rl)
        y = jnp.concatenate([ybuf[cb, rows, :] for cb in range(ncb)], axis=1)
        mu = jnp.mean(y, axis=-1, keepdims=True)
        yc = y - mu
        var = jnp.mean(yc * yc, axis=-1, keepdims=True)
        a = yc * lax.rsqrt(var + LN_EPS) * lng_ref[...] + lnb_ref[...]
        abuf[rows, :] = _silu(a).astype(BF16)
        return carry
    lax.fori_loop(0, t // rl, ln_body, 0)

    o = jnp.dot(abuf[...], wpw_ref[...], preferred_element_type=F32)
    out_ref[...] = (o * _silu(gate_ref[...])).astype(BF16)


def _conv_branch(zf, w_dw, b_dw, ln_g, ln_b, w_pw, seq):
    m = zf.shape[0]
    t = min(T_CONV, seq)
    tiles_per_seq = seq // t
    hb = t // HALO_ROWS
    n_hblocks = m // HALO_ROWS

    def main(col):
        return pl.BlockSpec((t, C_CONV), lambda i: (i, col))

    def prev(col):
        return pl.BlockSpec((HALO_ROWS, C_CONV), lambda i: (jnp.maximum(i * hb - 1, 0), col))

    def nxt(col):
        return pl.BlockSpec((HALO_ROWS, C_CONV), lambda i: (jnp.minimum((i + 1) * hb, n_hblocks - 1), col))

    def whole(shape):
        return pl.BlockSpec(shape, lambda i: (0, 0))

    return pl.pallas_call(
        functools.partial(_conv_kernel, t=t, tiles_per_seq=tiles_per_seq),
        out_shape=jax.ShapeDtypeStruct((m, C_CONV), BF16),
        grid=(m // t,),
        in_specs=[main(0), main(1), main(2), prev(0), prev(1), nxt(0), nxt(1),
                  whole((CONV_WIDTH + 1, C_CONV)), whole((1, C_CONV)), whole((1, C_CONV)),
                  whole((1, C_CONV)), whole((C_CONV, C_CONV))],
        out_specs=pl.BlockSpec((t, C_CONV), lambda i: (i, 0)),
        scratch_shapes=[pltpu.VMEM((C_CONV // LANES, t + 2 * HALO_ROWS, LANES), F32),
                        pltpu.VMEM((C_CONV // LANES, t, LANES), F32),
                        pltpu.VMEM((t, C_CONV), BF16)],
        compiler_params=pltpu.CompilerParams(
            dimension_semantics=(_ARB,), vmem_limit_bytes=VMEM_LIMIT),
        name="conv_branch",
    )(zf, zf, zf, zf, zf, zf, zf, w_dw, b_dw, ln_g, ln_b, w_pw)


def _attn_kernel(sink_ref, q_ref, k_ref, v_ref, kp_ref, vp_ref, kn_ref, vn_ref, gate_ref, out_ref,
                 kbuf, vbuf, *, t, tiles_per_seq):
    i = pl.program_id(0)
    t_in_seq = i % tiles_per_seq
    off_start = jnp.where(t_in_seq == 0, 2 * BLOCK, 0)
    off_end = jnp.where(t_in_seq == tiles_per_seq - 1, 2 * BLOCK, 0)

    kbuf[0:BLOCK, :] = kp_ref[...]
    kbuf[BLOCK:BLOCK + t, :] = k_ref[...]
    kbuf[BLOCK + t:2 * BLOCK + t, :] = kn_ref[...]
    vbuf[0:BLOCK, :] = vp_ref[...]
    vbuf[BLOCK:BLOCK + t, :] = v_ref[...]
    vbuf[BLOCK + t:2 * BLOCK + t, :] = vn_ref[...]

    qi = lax.broadcasted_iota(jnp.int32, (BLOCK, BLOCK), 0)
    kj = lax.broadcasted_iota(jnp.int32, (BLOCK, BLOCK), 1)
    nqb = t // BLOCK

    for qb in range(nqb):
        mask_prev = (kj >= qi + off_start) if qb == 0 else (kj >= qi)
        mask_next = (kj + off_end <= qi) if qb == nqb - 1 else (kj <= qi)
        qrows = slice(qb * BLOCK, (qb + 1) * BLOCK)
        krows = slice(qb * BLOCK, (qb + 3) * BLOCK)
        for h2 in range(N_KV_HEADS):
            kcols = slice(h2 * HEAD_DIM, (h2 + 1) * HEAD_DIM)
            kk = kbuf[krows, kcols]
            vv = vbuf[krows, kcols]
            qs = jnp.concatenate(
                [q_ref[qrows, (h2 * GROUP + g) * HEAD_DIM:(h2 * GROUP + g + 1) * HEAD_DIM]
                 for g in range(GROUP)], axis=0)
            s = lax.dot_general(qs, kk, (((1,), (1,)), ((), ())), preferred_element_type=F32)
            ps, rinv = [], []
            for g in range(GROUP):
                sink = sink_ref[h2 * GROUP + g]
                sg = s[g * BLOCK:(g + 1) * BLOCK, :]
                s0 = jnp.where(mask_prev, sg[:, 0:BLOCK], MASK_VALUE)
                s1 = sg[:, BLOCK:2 * BLOCK]
                s2 = jnp.where(mask_next, sg[:, 2 * BLOCK:3 * BLOCK], MASK_VALUE)
                mx = jnp.max(jnp.maximum(jnp.maximum(s0, s1), s2), axis=-1, keepdims=True)
                mx = jnp.maximum(mx, sink)
                p0, p1, p2 = jnp.exp(s0 - mx), jnp.exp(s1 - mx), jnp.exp(s2 - mx)
                denom = jnp.sum(p0 + p1 + p2, axis=-1, keepdims=True) + jnp.exp(sink - mx)
                rinv.append(1.0 / denom)
                ps.append(jnp.concatenate([p0, p1, p2], axis=1).astype(BF16))
            o = jnp.dot(jnp.concatenate(ps, axis=0), vv, preferred_element_type=F32)
            for g in range(GROUP):
                hcols = slice((h2 * GROUP + g) * HEAD_DIM, (h2 * GROUP + g + 1) * HEAD_DIM)
                og = o[g * BLOCK:(g + 1) * BLOCK, :] * rinv[g]
                out_ref[qrows, hcols] = (og * _silu(gate_ref[qrows, hcols])).astype(BF16)


def _attention(zf, zb, sink, seq):
    m = zf.shape[0]
    t = min(T_ATTN, seq)
    tiles_per_seq = seq // t
    bpt = t // BLOCK
    n_blocks = m // BLOCK
    kcol, vcol = D_ATTN // D_KV, D_ATTN // D_KV + 1

    def main(col):
        return pl.BlockSpec((t, D_KV), lambda i: (i, col))

    def prev(col):
        return pl.BlockSpec((BLOCK, D_KV), lambda i: (jnp.maximum(i * bpt - 1, 0), col))

    def nxt(col):
        return pl.BlockSpec((BLOCK, D_KV), lambda i: (jnp.minimum((i + 1) * bpt, n_blocks - 1), col))

    return pl.pallas_call(
        functools.partial(_attn_kernel, t=t, tiles_per_seq=tiles_per_seq),
        out_shape=jax.ShapeDtypeStruct((m, D_ATTN), BF16),
        grid=(m // t,),
        in_specs=[pl.BlockSpec(memory_space=pltpu.SMEM),
                  pl.BlockSpec((t, D_ATTN), lambda i: (i, 0)),
                  main(kcol), main(vcol), prev(kcol), prev(vcol), nxt(kcol), nxt(vcol),
                  pl.BlockSpec((t, D_ATTN), lambda i: (i, 3 * C_CONV // D_ATTN))],
        out_specs=pl.BlockSpec((t, D_ATTN), lambda i: (i, 0)),
        scratch_shapes=[pltpu.VMEM((t + 2 * BLOCK, D_KV), BF16),
                        pltpu.VMEM((t + 2 * BLOCK, D_KV), BF16)],
        compiler_params=pltpu.CompilerParams(
            dimension_semantics=(_ARB,), vmem_limit_bytes=VMEM_LIMIT),
        name="attention",
    )(sink, zb, zb, zb, zb, zb, zb, zb, zf)


def _outproj_kernel(c_ref, a_ref, w_ref, x_ref, g_ref, out_ref, acc_ref, *, tm, final_norm):
    acc_ref[...] = (x_ref[...]
                    + jnp.dot(c_ref[...], w_ref[0:C_CONV, :], preferred_element_type=F32)
                    + jnp.dot(a_ref[...], w_ref[C_CONV:C_CONV + D_ATTN, :], preferred_element_type=F32))
    rc = 64

    def body(r, carry):
        rows = pl.ds(pl.multiple_of(r * rc, rc), rc)
        y = acc_ref[rows, :]
        if final_norm:
            ms = jnp.mean(y * y, axis=-1, keepdims=True)
            y = y * lax.rsqrt(ms + NORM_EPS) * g_ref[...]
        out_ref[rows, :] = y
        return carry
    lax.fori_loop(0, tm // rc, body, 0)


def _outproj(conv_out, attn_out, w_out, x2, g, final_norm):
    m = x2.shape[0]
    tm = min(TM_OUT, m)
    return pl.pallas_call(
        functools.partial(_outproj_kernel, tm=tm, final_norm=final_norm),
        out_shape=jax.ShapeDtypeStruct((m, D_MODEL), F32),
        grid=(m // tm,),
        in_specs=[pl.BlockSpec((tm, C_CONV), lambda i: (i, 0)),
                  pl.BlockSpec((tm, D_ATTN), lambda i: (i, 0)),
                  pl.BlockSpec((C_CONV + D_ATTN, D_MODEL), lambda i: (0, 0)),
                  pl.BlockSpec((tm, D_MODEL), lambda i: (i, 0)),
                  pl.BlockSpec((1, D_MODEL), lambda i: (0, 0))],
        out_specs=pl.BlockSpec((tm, D_MODEL), lambda i: (i, 0)),
        scratch_shapes=[pltpu.VMEM((tm, D_MODEL), F32)],
        compiler_params=pltpu.CompilerParams(
            dimension_semantics=(_ARB,), vmem_limit_bytes=VMEM_LIMIT),
        name="outproj",
    )(conv_out, attn_out, w_out, x2, g)


def _rope_tables(seq):
    half = HEAD_DIM // 2
    inv_freq = 1.0 / (ROPE_THETA ** (jnp.arange(half, dtype=F32) / half))
    ang = jnp.arange(seq).astype(F32)[:, None] * inv_freq[None, :]
    cos, sin = jnp.cos(ang), jnp.sin(ang)
    cos_full = jnp.concatenate([cos, cos], axis=-1)
    sin_signed = jnp.concatenate([-sin, sin], axis=-1)
    scale = HEAD_DIM ** -0.5
    return (cos_full * scale, sin_signed * scale, cos_full, sin_signed)


def _prep_layer(w_in, w_dw, b_dw, ln_g, ln_b, w_pw, w_out):
    c3 = 3 * C_CONV
    qkv_end = c3 + D_ZB
    w_r = jnp.concatenate([w_in[:, :c3], w_in[:, qkv_end:], w_in[:, c3:qkv_end]], axis=1).astype(BF16)
    w_dw_p = jnp.concatenate([w_dw, jnp.zeros((1, C_CONV), F32)], axis=0)
    return (w_r, w_dw_p, b_dw.reshape(1, C_CONV), ln_g.reshape(1, C_CONV), ln_b.reshape(1, C_CONV),
            w_pw.astype(BF16), w_out.astype(BF16))


def _encoder(x, layers, norm_g, attn_sink, final_norm_g, tabs):
    b, seq, _ = x.shape
    x2 = x.reshape(b * seq, D_MODEL)
    depth = len(layers)
    for l, (w_r, w_dw_p, b_dw, ln_g, ln_b, w_pw, w_out) in enumerate(layers):
        zf, zb = _inproj(x2, norm_g[l].reshape(1, D_MODEL), w_r, tabs, seq)
        conv_out = _conv_branch(zf, w_dw_p, b_dw, ln_g, ln_b, w_pw, seq)
        attn_out = _attention(zf, zb, attn_sink[l], seq)
        x2 = _outproj(conv_out, attn_out, w_out, x2, final_norm_g.reshape(1, D_MODEL),
                      final_norm=(l == depth - 1))
    return x2.reshape(b, seq, D_MODEL)


def kernel(x_prompt, x_sample, norm_g, w_in, w_dw, b_dw, conv_ln_g, conv_ln_b, w_pw, attn_sink, w_out,
           final_norm_g):
    depth = w_in.shape[0]
    layers = [_prep_layer(w_in[l], w_dw[l], b_dw[l], conv_ln_g[l], conv_ln_b[l], w_pw[l], w_out[l])
              for l in range(depth)]
    outs = []
    for x in (x_prompt, x_sample):
        tabs = _rope_tables(x.shape[1])
        outs.append(_encoder(x, layers, norm_g, attn_sink, final_norm_g, tabs))
    return tuple(outs)
```

```python
import functools

import jax
import jax.numpy as jnp
from jax import lax
from jax.experimental import pallas as pl
from jax.experimental.pallas import tpu as pltpu

D_MODEL = 2048
C_CONV = 1024
CONV_WIDTH = 31
CONV_PAD = CONV_WIDTH // 2
HEAD_DIM = 128
N_HEADS = 8
N_KV_HEADS = 2
GROUP = N_HEADS // N_KV_HEADS
BLOCK = 128
ROPE_THETA = 10000.0
NORM_EPS = 1e-6
LN_EPS = 1e-5
D_ATTN = N_HEADS * HEAD_DIM
D_KV = N_KV_HEADS * HEAD_DIM
D_IN_PROJ = 3 * C_CONV + 2 * D_ATTN + 2 * D_KV
MASK_VALUE = -1e30

F32 = jnp.float32
BF16 = jnp.bfloat16

LANES = 128
N_CB = C_CONV // LANES
HALO_ROWS = 16
TM_PROJ = 512
GLU_COLS = 256
T_CONV = 512
T_ATTN = 512
VMEM_LIMIT = 58 * 1024 * 1024

_ARB = pltpu.ARBITRARY


def _silu(x):
    return x * jax.nn.sigmoid(x)


def _resident(shape):
    return pl.BlockSpec(shape, lambda i: (0,) * len(shape), pipeline_mode=pl.Buffered(1))


def _inproj_kernel(x_ref, g_ref, w_ref, cq_ref, sq_ref, ck_ref, sk_ref,
                   h_ref, sg_ref, q_ref, kv_ref, xb_ref):
    x = x_ref[...]
    rs = lax.rsqrt(jnp.mean(x * x, axis=-1, keepdims=True) + NORM_EPS)
    xb_ref[...] = (x * g_ref[...]).astype(BF16)

    def proj(c0, n):
        return jnp.dot(xb_ref[...], w_ref[:, c0:c0 + n], preferred_element_type=F32) * rs

    def rope(t, c_ref, s_ref):
        return t * c_ref[...] + pltpu.roll(t, HEAD_DIM // 2, 1) * s_ref[...]

    for c in range(C_CONV // GLU_COLS):
        val = proj(c * GLU_COLS, GLU_COLS)
        glu = proj(C_CONV + c * GLU_COLS, GLU_COLS)
        h = val * jax.nn.sigmoid(glu)
        for s in range(GLU_COLS // LANES):
            h_ref[c * (GLU_COLS // LANES) + s, :, :] = h[:, s * LANES:(s + 1) * LANES]

    half = C_CONV // 2
    for c in range(2 * C_CONV // half):
        sg_ref[:, c * half:(c + 1) * half] = _silu(proj(2 * C_CONV + c * half, half))

    q0 = 2 * C_CONV + C_CONV + D_ATTN
    for c in range(D_ATTN // half):
        z = proj(q0 + c * half, half)
        for hh in range(half // HEAD_DIM):
            cols = slice(hh * HEAD_DIM, (hh + 1) * HEAD_DIM)
            q_ref[:, c * half + hh * HEAD_DIM:c * half + (hh + 1) * HEAD_DIM] = (
                rope(z[:, cols], cq_ref, sq_ref).astype(BF16))

    z = proj(q0 + D_ATTN, 2 * D_KV)
    for hh in range(N_KV_HEADS):
        cols = slice(hh * HEAD_DIM, (hh + 1) * HEAD_DIM)
        kv_ref[:, cols] = rope(z[:, cols], ck_ref, sk_ref).astype(BF16)
    kv_ref[:, D_KV:2 * D_KV] = z[:, D_KV:2 * D_KV].astype(BF16)


def _inproj(x2, g, w, tabs, seq):
    m = x2.shape[0]
    tm = min(TM_PROJ, seq)
    tiles_per_seq = seq // tm
    tab_spec = pl.BlockSpec((tm, HEAD_DIM), lambda i: (i % tiles_per_seq, 0))
    return pl.pallas_call(
        _inproj_kernel,
        out_shape=(jax.ShapeDtypeStruct((N_CB, m, LANES), F32),
                   jax.ShapeDtypeStruct((m, 2 * C_CONV), F32),
                   jax.ShapeDtypeStruct((m, D_ATTN), BF16),
                   jax.ShapeDtypeStruct((m, 2 * D_KV), BF16)),
        grid=(m // tm,),
        in_specs=[pl.BlockSpec((tm, D_MODEL), lambda i: (i, 0)),
                  _resident((1, D_MODEL)),
                  _resident((D_MODEL, D_IN_PROJ)),
                  tab_spec, tab_spec, tab_spec, tab_spec],
        out_specs=(pl.BlockSpec((N_CB, tm, LANES), lambda i: (0, i, 0)),
                   pl.BlockSpec((tm, 2 * C_CONV), lambda i: (i, 0)),
                   pl.BlockSpec((tm, D_ATTN), lambda i: (i, 0)),
                   pl.BlockSpec((tm, 2 * D_KV), lambda i: (i, 0))),
        scratch_shapes=[pltpu.VMEM((tm, D_MODEL), BF16)],
        compiler_params=pltpu.CompilerParams(
            dimension_semantics=(_ARB,), vmem_limit_bytes=VMEM_LIMIT),
        name="inproj",
    )(x2, g, w, *tabs)


def _conv_kernel(h_ref, hp_ref, hn_ref, sg_ref, wdw_ref, bdw_ref, lng_ref, lnb_ref, wpw_ref, out_ref,
                 hbuf, ybuf, *, t, tiles_per_seq):
    i = pl.program_id(0)
    t_in_seq = i % tiles_per_seq
    at_start = t_in_seq == 0
    at_end = t_in_seq == tiles_per_seq - 1

    for cb in range(N_CB):
        hbuf[cb, 0:HALO_ROWS, :] = jnp.where(at_start, 0.0, hp_ref[cb])
        hbuf[cb, HALO_ROWS:t + HALO_ROWS, :] = h_ref[cb]
        hbuf[cb, t + HALO_ROWS:t + 2 * HALO_ROWS, :] = jnp.where(at_end, 0.0, hn_ref[cb])

    rb = 64
    shift = HALO_ROWS - CONV_PAD

    def conv_body(r, carry):
        r0 = pl.multiple_of(r * (2 * rb), 2 * rb)
        for cb in range(N_CB):
            lanes = slice(cb * LANES, (cb + 1) * LANES)
            acc_e = jnp.broadcast_to(bdw_ref[:, lanes], (rb, LANES))
            acc_o = acc_e
            for k in range(CONV_WIDTH):
                w = wdw_ref[k:k + 1, lanes]
                acc_e = acc_e + w * hbuf[cb, pl.ds(r0 + (k + shift), rb, stride=2), :]
                acc_o = acc_o + w * hbuf[cb, pl.ds(r0 + (k + shift + 1), rb, stride=2), :]
            ybuf[cb, pl.ds(r0, rb, stride=2), :] = acc_e
            ybuf[cb, pl.ds(r0 + 1, rb, stride=2), :] = acc_o
        return carry
    lax.fori_loop(0, t // (2 * rb), conv_body, 0)

    y = jnp.concatenate([ybuf[cb] for cb in range(N_CB)], axis=1)
    mu = jnp.mean(y, axis=-1, keepdims=True)
    yc = y - mu
    var = jnp.mean(yc * yc, axis=-1, keepdims=True)
    a = _silu(yc * lax.rsqrt(var + LN_EPS) * lng_ref[...] + lnb_ref[...]).astype(BF16)
    o = jnp.dot(a, wpw_ref[...], preferred_element_type=F32)
    out_ref[...] = (o * sg_ref[...]).astype(BF16)


def _conv_branch(h, sg, w_dw, b_dw, ln_g, ln_b, w_pw, seq):
    m = sg.shape[0]
    t = min(T_CONV, seq)
    tiles_per_seq = seq // t
    hb = t // HALO_ROWS
    n_hblocks = m // HALO_ROWS
    return pl.pallas_call(
        functools.partial(_conv_kernel, t=t, tiles_per_seq=tiles_per_seq),
        out_shape=jax.ShapeDtypeStruct((m, C_CONV), BF16),
        grid=(m // t,),
        in_specs=[pl.BlockSpec((N_CB, t, LANES), lambda i: (0, i, 0)),
                  pl.BlockSpec((N_CB, HALO_ROWS, LANES), lambda i: (0, jnp.maximum(i * hb - 1, 0), 0)),
                  pl.BlockSpec((N_CB, HALO_ROWS, LANES),
                               lambda i: (0, jnp.minimum((i + 1) * hb, n_hblocks - 1), 0)),
                  pl.BlockSpec((t, C_CONV), lambda i: (i, 0)),
                  _resident((CONV_WIDTH + 1, C_CONV)), _resident((1, C_CONV)), _resident((1, C_CONV)),
                  _resident((1, C_CONV)), _resident((C_CONV, C_CONV))],
        out_specs=pl.BlockSpec((t, C_CONV), lambda i: (i, 0)),
        scratch_shapes=[pltpu.VMEM((N_CB, t + 2 * HALO_ROWS, LANES), F32),
                        pltpu.VMEM((N_CB, t, LANES), F32)],
        compiler_params=pltpu.CompilerParams(
            dimension_semantics=(_ARB,), vmem_limit_bytes=VMEM_LIMIT),
        name="conv_branch",
    )(h, h, h, sg, w_dw, b_dw, ln_g, ln_b, w_pw)


def _attn_out_kernel(sink_ref, q_ref, kv_ref, kvp_ref, kvn_ref, sg_ref, c_ref, w_ref, x_ref, g_ref,
                     out_ref, kvbuf, *, t, tiles_per_seq, final_norm):
    i = pl.program_id(0)
    t_in_seq = i % tiles_per_seq
    off_start = jnp.where(t_in_seq == 0, 2 * BLOCK, 0)
    off_end = jnp.where(t_in_seq == tiles_per_seq - 1, 2 * BLOCK, 0)

    kvbuf[0:BLOCK, :] = kvp_ref[...]
    kvbuf[BLOCK:BLOCK + t, :] = kv_ref[...]
    kvbuf[BLOCK + t:2 * BLOCK + t, :] = kvn_ref[...]

    qi = lax.broadcasted_iota(jnp.int32, (BLOCK, BLOCK), 0)
    kj = lax.broadcasted_iota(jnp.int32, (BLOCK, BLOCK), 1)
    nqb = t // BLOCK

    rows_out = []
    for qb in range(nqb):
        mask_prev = (kj >= qi + off_start) if qb == 0 else (kj >= qi)
        mask_next = (kj + off_end <= qi) if qb == nqb - 1 else (kj <= qi)
        qrows = slice(qb * BLOCK, (qb + 1) * BLOCK)
        krows = slice(qb * BLOCK, (qb + 3) * BLOCK)
        heads_out = []
        for h2 in range(N_KV_HEADS):
            kk = kvbuf[krows, h2 * HEAD_DIM:(h2 + 1) * HEAD_DIM]
            vv = kvbuf[krows, D_KV + h2 * HEAD_DIM:D_KV + (h2 + 1) * HEAD_DIM]
            qs = jnp.concatenate(
                [q_ref[qrows, (h2 * GROUP + g) * HEAD_DIM:(h2 * GROUP + g + 1) * HEAD_DIM]
                 for g in range(GROUP)], axis=0)
            s = lax.dot_general(qs, kk, (((1,), (1,)), ((), ())), preferred_element_type=F32)
            ps, rinv = [], []
            for g in range(GROUP):
                sink = sink_ref[h2 * GROUP + g]
                sg = s[g * BLOCK:(g + 1) * BLOCK, :]
                s0 = jnp.where(mask_prev, sg[:, 0:BLOCK], MASK_VALUE)
                s1 = sg[:, BLOCK:2 * BLOCK]
                s2 = jnp.where(mask_next, sg[:, 2 * BLOCK:3 * BLOCK], MASK_VALUE)
                mx = jnp.max(jnp.maximum(jnp.maximum(s0, s1), s2), axis=-1, keepdims=True)
                mx = jnp.maximum(mx, sink)
                p0, p1, p2 = jnp.exp(s0 - mx), jnp.exp(s1 - mx), jnp.exp(s2 - mx)
                denom = jnp.sum(p0 + p1 + p2, axis=-1, keepdims=True) + jnp.exp(sink - mx)
                rinv.append(1.0 / denom)
                ps.append(jnp.concatenate([p0, p1, p2], axis=1).astype(BF16))
            o = jnp.dot(jnp.concatenate(ps, axis=0), vv, preferred_element_type=F32)
            for g in range(GROUP):
                hcols = slice((h2 * GROUP + g) * HEAD_DIM, (h2 * GROUP + g + 1) * HEAD_DIM)
                og = o[g * BLOCK:(g + 1) * BLOCK, :] * rinv[g]
                heads_out.append((og * sg_ref[qrows, hcols]).astype(BF16))
        rows_out.append(jnp.concatenate(heads_out, axis=1))
    attn = jnp.concatenate(rows_out, axis=0)

    y = (x_ref[...]
         + jnp.dot(c_ref[...], w_ref[0:C_CONV, :], preferred_element_type=F32)
         + jnp.dot(attn, w_ref[C_CONV:C_CONV + D_ATTN, :], preferred_element_type=F32))
    if final_norm:
        y = y * lax.rsqrt(jnp.mean(y * y, axis=-1, keepdims=True) + NORM_EPS) * g_ref[...]
    out_ref[...] = y


def _attention_outproj(q, kv, sg, conv_out, sink, w_out, x2, g, seq, final_norm):
    m = x2.shape[0]
    t = min(T_ATTN, seq)
    tiles_per_seq = seq // t
    bpt = t // BLOCK
    n_blocks = m // BLOCK
    return pl.pallas_call(
        functools.partial(_attn_out_kernel, t=t, tiles_per_seq=tiles_per_seq, final_norm=final_norm),
        out_shape=jax.ShapeDtypeStruct((m, D_MODEL), F32),
        grid=(m // t,),
        in_specs=[pl.BlockSpec(memory_space=pltpu.SMEM),
                  pl.BlockSpec((t, D_ATTN), lambda i: (i, 0)),
                  pl.BlockSpec((t, 2 * D_KV), lambda i: (i, 0)),
                  pl.BlockSpec((BLOCK, 2 * D_KV), lambda i: (jnp.maximum(i * bpt - 1, 0), 0)),
                  pl.BlockSpec((BLOCK, 2 * D_KV), lambda i: (jnp.minimum((i + 1) * bpt, n_blocks - 1), 0)),
                  pl.BlockSpec((t, D_ATTN), lambda i: (i, 1)),
                  pl.BlockSpec((t, C_CONV), lambda i: (i, 0)),
                  _resident((C_CONV + D_ATTN, D_MODEL)),
                  pl.BlockSpec((t, D_MODEL), lambda i: (i, 0)),
                  _resident((1, D_MODEL))],
        out_specs=pl.BlockSpec((t, D_MODEL), lambda i: (i, 0)),
        scratch_shapes=[pltpu.VMEM((t + 2 * BLOCK, 2 * D_KV), BF16)],
        compiler_params=pltpu.CompilerParams(
            dimension_semantics=(_ARB,), vmem_limit_bytes=VMEM_LIMIT),
        name="attn_outproj",
    )(sink, q, kv, kv, kv, sg, conv_out, w_out, x2, g)


def _rope_tables(seq):
    half = HEAD_DIM // 2
    inv_freq = 1.0 / (ROPE_THETA ** (jnp.arange(half, dtype=F32) / half))
    ang = jnp.arange(seq).astype(F32)[:, None] * inv_freq[None, :]
    cos, sin = jnp.cos(ang), jnp.sin(ang)
    cos_full = jnp.concatenate([cos, cos], axis=-1)
    sin_signed = jnp.concatenate([-sin, sin], axis=-1)
    scale = HEAD_DIM ** -0.5
    return (cos_full * scale, sin_signed * scale, cos_full, sin_signed)


def _prep_layer(w_in, w_dw, b_dw, ln_g, ln_b, w_pw, w_out):
    c3 = 3 * C_CONV
    qkv_end = c3 + D_ATTN + 2 * D_KV
    w_r = jnp.concatenate([w_in[:, :c3], w_in[:, qkv_end:], w_in[:, c3:qkv_end]], axis=1).astype(BF16)
    w_dw_p = jnp.concatenate([w_dw, jnp.zeros((1, C_CONV), F32)], axis=0)
    return (w_r, w_dw_p, b_dw.reshape(1, C_CONV), ln_g.reshape(1, C_CONV), ln_b.reshape(1, C_CONV),
            w_pw.astype(BF16), w_out.astype(BF16))


def _encoder(x, layers, norm_g, attn_sink, final_norm_g, tabs):
    b, seq, _ = x.shape
    x2 = x.reshape(b * seq, D_MODEL)
    depth = len(layers)
    for l, (w_r, w_dw_p, b_dw, ln_g, ln_b, w_pw, w_out) in enumerate(layers):
        h, sg, q, kv = _inproj(x2, norm_g[l].reshape(1, D_MODEL), w_r, tabs, seq)
        conv_out = _conv_branch(h, sg, w_dw_p, b_dw, ln_g, ln_b, w_pw, seq)
        x2 = _attention_outproj(q, kv, sg, conv_out, attn_sink[l], w_out, x2,
                                final_norm_g.reshape(1, D_MODEL), seq, final_norm=(l == depth - 1))
    return x2.reshape(b, seq, D_MODEL)


def kernel(x_prompt, x_sample, norm_g, w_in, w_dw, b_dw, conv_ln_g, conv_ln_b, w_pw, attn_sink, w_out,
           final_norm_g):
    depth = w_in.shape[0]
    layers = [_prep_layer(w_in[l], w_dw[l], b_dw[l], conv_ln_g[l], conv_ln_b[l], w_pw[l], w_out[l])
              for l in range(depth)]
    outs = []
    for x in (x_prompt, x_sample):
        tabs = _rope_tables(x.shape[1])
        outs.append(_encoder(x, layers, norm_g, attn_sink, final_norm_g, tabs))
    return tuple(outs)
```

```python
import functools

import jax
import jax.numpy as jnp
from jax import lax
from jax.experimental import pallas as pl
from jax.experimental.pallas import tpu as pltpu

D_MODEL = 2048
C_CONV = 1024
CONV_WIDTH = 31
CONV_PAD = CONV_WIDTH // 2
HEAD_DIM = 128
N_HEADS = 8
N_KV_HEADS = 2
GROUP = N_HEADS // N_KV_HEADS
BLOCK = 128
ROPE_THETA = 10000.0
NORM_EPS = 1e-6
LN_EPS = 1e-5
D_ATTN = N_HEADS * HEAD_DIM
D_KV = N_KV_HEADS * HEAD_DIM
D_IN_PROJ = 3 * C_CONV + 2 * D_ATTN + 2 * D_KV
MASK_VALUE = -1e30
LOG2_E = 1.4426950408889634

F32 = jnp.float32
BF16 = jnp.bfloat16

LANES = 128
N_CB = C_CONV // LANES
HALO_ROWS = 16
TM_GLU = 1024
TM_PROJ = 512
W_TAIL_COLS = 512
GLU_COLS = 256
CONV_ROWS = 32
T_ATTN = 512
VMEM_LIMIT = 58 * 1024 * 1024

COL_VAL = 0
COL_GLU = C_CONV
COL_CGATE = 2 * C_CONV
COL_Q = 3 * C_CONV
COL_KV = COL_Q + D_ATTN
COL_AGATE = COL_KV + 2 * D_KV

_ARB = pltpu.ARBITRARY


def _silu(x):
    return x * jax.nn.sigmoid(x)


def _resident(shape, index=None):
    index = (0,) * len(shape) if index is None else index
    return pl.BlockSpec(shape, lambda i: index, pipeline_mode=pl.Buffered(1))


def _glu_kernel(x_ref, g_ref, w_ref, h_ref, xn_ref):
    x = x_ref[...]
    rs = lax.rsqrt(jnp.mean(x * x, axis=-1, keepdims=True) + NORM_EPS)
    xn_ref[...] = (x * rs * g_ref[...]).astype(BF16)
    for c in range(C_CONV // GLU_COLS):
        val = jnp.dot(xn_ref[...], w_ref[:, COL_VAL + c * GLU_COLS:COL_VAL + (c + 1) * GLU_COLS],
                      preferred_element_type=F32)
        glu = jnp.dot(xn_ref[...], w_ref[:, COL_GLU + c * GLU_COLS:COL_GLU + (c + 1) * GLU_COLS],
                      preferred_element_type=F32)
        h = val * jax.nn.sigmoid(glu)
        for s in range(GLU_COLS // LANES):
            h_ref[c * (GLU_COLS // LANES) + s, :, :] = h[:, s * LANES:(s + 1) * LANES]


def _glu_proj(x2, g, w, seq):
    m = x2.shape[0]
    tm = min(TM_GLU, seq)
    return pl.pallas_call(
        _glu_kernel,
        out_shape=(jax.ShapeDtypeStruct((N_CB, m, LANES), F32), jax.ShapeDtypeStruct((m, D_MODEL), BF16)),
        grid=(m // tm,),
        in_specs=[pl.BlockSpec((tm, D_MODEL), lambda i: (i, 0)),
                  _resident((1, D_MODEL)),
                  _resident((D_MODEL, 2 * C_CONV))],
        out_specs=(pl.BlockSpec((N_CB, tm, LANES), lambda i: (0, i, 0)),
                   pl.BlockSpec((tm, D_MODEL), lambda i: (i, 0))),
        compiler_params=pltpu.CompilerParams(
            dimension_semantics=(_ARB,), vmem_limit_bytes=VMEM_LIMIT),
        name="glu_proj",
    )(x2, g, w)


def _chain_zero(v):
    return jnp.minimum(jnp.abs(jnp.max(v, axis=1, keepdims=True)), 0.0)


def _proj_conv_kernel(xn_ref, wa_ref, wb_ref, wc_ref, wd_ref, cos_ref, sin_ref, h_ref, hp_ref, hn_ref,
                      wdw_ref, bdw_ref, lng_ref, lnb_ref, wpw_ref,
                      sga_ref, q_ref, kv_ref, co_ref,
                      hbuf, ybuf, sgc_ref, *, tm, tiles_per_seq):
    i = pl.program_id(0)
    t_in_seq = i % tiles_per_seq
    at_start = t_in_seq == 0
    at_end = t_in_seq == tiles_per_seq - 1
    hr = HALO_ROWS

    for cb in range(N_CB):
        hbuf[cb, 0:hr, :] = jnp.where(at_start, 0.0, hp_ref[cb])
        hbuf[cb, hr:hr + tm, :] = h_ref[cb]
        hbuf[cb, hr + tm:tm + 2 * hr, :] = jnp.where(at_end, 0.0, hn_ref[cb])

    rb = CONV_ROWS
    shift = hr - CONV_PAD
    dep = jnp.zeros((8, LANES), F32)
    for cb in range(N_CB):
        lanes = slice(cb * LANES, (cb + 1) * LANES)
        for r in range(tm // (2 * rb)):
            r0 = r * 2 * rb
            acc_e = jnp.broadcast_to(bdw_ref[:, lanes], (rb, LANES))
            acc_o = acc_e
            for k in range(CONV_WIDTH):
                w = jnp.broadcast_to(wdw_ref[k:k + 1, lanes], (8, LANES)) + dep
                w = jnp.concatenate([w] * (rb // 8), axis=0)
                acc_e = acc_e + w * hbuf[cb, pl.ds(r0 + k + shift, rb, stride=2), :]
                acc_o = acc_o + w * hbuf[cb, pl.ds(r0 + k + shift + 1, rb, stride=2), :]
            ybuf[cb, pl.ds(r0, rb, stride=2), :] = acc_e
            ybuf[cb, pl.ds(r0 + 1, rb, stride=2), :] = acc_o
            dep = jnp.broadcast_to(_chain_zero(acc_e[0:8, :] + acc_o[0:8, :]), (8, LANES))

    def proj(c0, n):
        for ref, base, width in ((wa_ref, COL_CGATE, COL_KV - COL_CGATE), (wb_ref, COL_KV, W_TAIL_COLS),
                                 (wc_ref, COL_AGATE, W_TAIL_COLS), (wd_ref, COL_AGATE + W_TAIL_COLS, W_TAIL_COLS)):
            if base <= c0 and c0 + n <= base + width:
                return jnp.dot(xn_ref[...], ref[:, c0 - base:c0 - base + n], preferred_element_type=F32)
        raise ValueError((c0, n))

    def rope(t):
        return t * cos_ref[...] + pltpu.roll(t, HEAD_DIM // 2, 1) * sin_ref[...]

    half = C_CONV // 2
    for c in range(D_ATTN // half):
        sga_ref[:, c * half:(c + 1) * half] = _silu(proj(COL_AGATE + c * half, half))

    scale = HEAD_DIM ** -0.5 * LOG2_E
    for c in range(D_ATTN // half):
        z = proj(COL_Q + c * half, half)
        for hh in range(half // HEAD_DIM):
            cols = slice(hh * HEAD_DIM, (hh + 1) * HEAD_DIM)
            q_ref[:, c * half + hh * HEAD_DIM:c * half + (hh + 1) * HEAD_DIM] = (
                (rope(z[:, cols]) * scale).astype(BF16))

    z = proj(COL_KV, 2 * D_KV)
    for hh in range(N_KV_HEADS):
        cols = slice(hh * HEAD_DIM, (hh + 1) * HEAD_DIM)
        kv_ref[:, cols] = rope(z[:, cols]).astype(BF16)
    kv_ref[:, D_KV:2 * D_KV] = z[:, D_KV:2 * D_KV].astype(BF16)

    for c in range(C_CONV // half):
        sgc_ref[:, c * half:(c + 1) * half] = _silu(proj(COL_CGATE + c * half, half))

    y = jnp.concatenate([ybuf[cb] for cb in range(N_CB)], axis=1)
    mu = jnp.mean(y, axis=-1, keepdims=True)
    yc = y - mu
    var = jnp.mean(yc * yc, axis=-1, keepdims=True)
    a = _silu(yc * lax.rsqrt(var + LN_EPS) * lng_ref[...] + lnb_ref[...]).astype(BF16)
    o = jnp.dot(a, wpw_ref[...], preferred_element_type=F32)
    co_ref[...] = (o * sgc_ref[...]).astype(BF16)


def _proj_conv(xn, w, tabs, h, w_dw, b_dw, ln_g, ln_b, w_pw, seq):
    m = xn.shape[0]
    tm = min(TM_PROJ, seq)
    tiles_per_seq = seq // tm
    hb = tm // HALO_ROWS
    n_hblocks = m // HALO_ROWS
    tab_spec = pl.BlockSpec((tm, HEAD_DIM), lambda i: (i % tiles_per_seq, 0))
    return pl.pallas_call(
        functools.partial(_proj_conv_kernel, tm=tm, tiles_per_seq=tiles_per_seq),
        out_shape=(jax.ShapeDtypeStruct((m, D_ATTN), F32),
                   jax.ShapeDtypeStruct((m, D_ATTN), BF16),
                   jax.ShapeDtypeStruct((m, 2 * D_KV), BF16),
                   jax.ShapeDtypeStruct((m, C_CONV), BF16)),
        grid=(m // tm,),
        in_specs=[pl.BlockSpec((tm, D_MODEL), lambda i: (i, 0)),
                  _resident((D_MODEL, COL_KV - COL_CGATE), (0, COL_CGATE // (COL_KV - COL_CGATE))),
                  _resident((D_MODEL, W_TAIL_COLS), (0, COL_KV // W_TAIL_COLS)),
                  _resident((D_MODEL, W_TAIL_COLS), (0, COL_AGATE // W_TAIL_COLS)),
                  _resident((D_MODEL, W_TAIL_COLS), (0, COL_AGATE // W_TAIL_COLS + 1)),
                  tab_spec, tab_spec,
                  pl.BlockSpec((N_CB, tm, LANES), lambda i: (0, i, 0)),
                  pl.BlockSpec((N_CB, HALO_ROWS, LANES), lambda i: (0, jnp.maximum(i * hb - 1, 0), 0)),
                  pl.BlockSpec((N_CB, HALO_ROWS, LANES),
                               lambda i: (0, jnp.minimum((i + 1) * hb, n_hblocks - 1), 0)),
                  _resident((CONV_WIDTH + 1, C_CONV)), _resident((1, C_CONV)), _resident((1, C_CONV)),
                  _resident((1, C_CONV)), _resident((C_CONV, C_CONV))],
        out_specs=(pl.BlockSpec((tm, D_ATTN), lambda i: (i, 0)),
                   pl.BlockSpec((tm, D_ATTN), lambda i: (i, 0)),
                   pl.BlockSpec((tm, 2 * D_KV), lambda i: (i, 0)),
                   pl.BlockSpec((tm, C_CONV), lambda i: (i, 0))),
        scratch_shapes=[pltpu.VMEM((N_CB, tm + 2 * HALO_ROWS, LANES), F32),
                        pltpu.VMEM((N_CB, tm, LANES), F32),
                        pltpu.VMEM((tm, C_CONV), F32)],
        compiler_params=pltpu.CompilerParams(
            dimension_semantics=(_ARB,), vmem_limit_bytes=VMEM_LIMIT),
        name="proj_conv",
    )(xn, w, w, w, w, *tabs, h, h, h, w_dw, b_dw, ln_g, ln_b, w_pw)


def _attn_out_kernel(sink_ref, q_ref, kv_ref, kvp_ref, kvn_ref, sg_ref, c_ref, w_ref, x_ref, g_ref,
                     out_ref, kvbuf, *, t, tiles_per_seq, final_norm):
    i = pl.program_id(0)
    t_in_seq = i % tiles_per_seq
    off_start = jnp.where(t_in_seq == 0, 2 * BLOCK, 0)
    off_end = jnp.where(t_in_seq == tiles_per_seq - 1, 2 * BLOCK, 0)

    kvbuf[0:BLOCK, :] = kvp_ref[...]
    kvbuf[BLOCK:BLOCK + t, :] = kv_ref[...]
    kvbuf[BLOCK + t:2 * BLOCK + t, :] = kvn_ref[...]

    qi = lax.broadcasted_iota(jnp.int32, (BLOCK, BLOCK), 0)
    kj = lax.broadcasted_iota(jnp.int32, (BLOCK, BLOCK), 1)
    nqb = t // BLOCK

    rows_out = []
    for qb in range(nqb):
        mask_prev = (kj >= qi + off_start) if qb == 0 else (kj >= qi)
        mask_next = (kj + off_end <= qi) if qb == nqb - 1 else (kj <= qi)
        qrows = slice(qb * BLOCK, (qb + 1) * BLOCK)
        krows = slice(qb * BLOCK, (qb + 3) * BLOCK)
        heads_out = []
        for h2 in range(N_KV_HEADS):
            kk = kvbuf[krows, h2 * HEAD_DIM:(h2 + 1) * HEAD_DIM]
            vv = kvbuf[krows, D_KV + h2 * HEAD_DIM:D_KV + (h2 + 1) * HEAD_DIM]
            qs = jnp.concatenate(
                [q_ref[qrows, (h2 * GROUP + g) * HEAD_DIM:(h2 * GROUP + g + 1) * HEAD_DIM]
                 for g in range(GROUP)], axis=0)
            s = lax.dot_general(qs, kk, (((1,), (1,)), ((), ())), preferred_element_type=F32)
            ps, rinv = [], []
            for g in range(GROUP):
                sink = sink_ref[h2 * GROUP + g] * LOG2_E
                sg = s[g * BLOCK:(g + 1) * BLOCK, :]
                s0 = jnp.where(mask_prev, sg[:, 0:BLOCK], MASK_VALUE)
                s1 = sg[:, BLOCK:2 * BLOCK]
                s2 = jnp.where(mask_next, sg[:, 2 * BLOCK:3 * BLOCK], MASK_VALUE)
                mx = jnp.max(jnp.maximum(jnp.maximum(s0, s1), s2), axis=-1, keepdims=True)
                mx = jnp.maximum(mx, sink)
                p0, p1, p2 = jnp.exp2(s0 - mx), jnp.exp2(s1 - mx), jnp.exp2(s2 - mx)
                denom = jnp.sum(p0 + p1 + p2, axis=-1, keepdims=True) + jnp.exp2(sink - mx)
                rinv.append(1.0 / denom)
                ps.append(jnp.concatenate([p0, p1, p2], axis=1).astype(BF16))
            o = jnp.dot(jnp.concatenate(ps, axis=0), vv, preferred_element_type=F32)
            for g in range(GROUP):
                hcols = slice((h2 * GROUP + g) * HEAD_DIM, (h2 * GROUP + g + 1) * HEAD_DIM)
                og = o[g * BLOCK:(g + 1) * BLOCK, :] * rinv[g]
                heads_out.append((og * sg_ref[qrows, hcols]).astype(BF16))
        rows_out.append(jnp.concatenate(heads_out, axis=1))
    attn = jnp.concatenate(rows_out, axis=0)

    y = (x_ref[...]
         + jnp.dot(c_ref[...], w_ref[0:C_CONV, :], preferred_element_type=F32)
         + jnp.dot(attn, w_ref[C_CONV:C_CONV + D_ATTN, :], preferred_element_type=F32))
    if final_norm:
        y = y * lax.rsqrt(jnp.mean(y * y, axis=-1, keepdims=True) + NORM_EPS) * g_ref[...]
    out_ref[...] = y


def _attention_outproj(q, kv, sg, conv_out, sink, w_out, x2, g, seq, final_norm):
    m = x2.shape[0]
    t = min(T_ATTN, seq)
    tiles_per_seq = seq // t
    bpt = t // BLOCK
    n_blocks = m // BLOCK
    return pl.pallas_call(
        functools.partial(_attn_out_kernel, t=t, tiles_per_seq=tiles_per_seq, final_norm=final_norm),
        out_shape=jax.ShapeDtypeStruct((m, D_MODEL), F32),
        grid=(m // t,),
        in_specs=[pl.BlockSpec(memory_space=pltpu.SMEM),
                  pl.BlockSpec((t, D_ATTN), lambda i: (i, 0)),
                  pl.BlockSpec((t, 2 * D_KV), lambda i: (i, 0)),
                  pl.BlockSpec((BLOCK, 2 * D_KV), lambda i: (jnp.maximum(i * bpt - 1, 0), 0)),
                  pl.BlockSpec((BLOCK, 2 * D_KV), lambda i: (jnp.minimum((i + 1) * bpt, n_blocks - 1), 0)),
                  pl.BlockSpec((t, D_ATTN), lambda i: (i, 0)),
                  pl.BlockSpec((t, C_CONV), lambda i: (i, 0)),
                  _resident((C_CONV + D_ATTN, D_MODEL)),
                  pl.BlockSpec((t, D_MODEL), lambda i: (i, 0)),
                  _resident((1, D_MODEL))],
        out_specs=pl.BlockSpec((t, D_MODEL), lambda i: (i, 0)),
        scratch_shapes=[pltpu.VMEM((t + 2 * BLOCK, 2 * D_KV), BF16)],
        compiler_params=pltpu.CompilerParams(
            dimension_semantics=(_ARB,), vmem_limit_bytes=VMEM_LIMIT),
        name="attn_outproj",
    )(sink, q, kv, kv, kv, sg, conv_out, w_out, x2, g)


def _rope_tables(seq):
    half = HEAD_DIM // 2
    inv_freq = 1.0 / (ROPE_THETA ** (jnp.arange(half, dtype=F32) / half))
    ang = jnp.arange(seq).astype(F32)[:, None] * inv_freq[None, :]
    cos, sin = jnp.cos(ang), jnp.sin(ang)
    cos_full = jnp.concatenate([cos, cos], axis=-1)
    sin_signed = jnp.concatenate([-sin, sin], axis=-1)
    return (cos_full, sin_signed)


def _prep_layer(w_in, w_dw, b_dw, ln_g, ln_b, w_pw, w_out):
    w_dw_p = jnp.concatenate([w_dw, jnp.zeros((1, C_CONV), F32)], axis=0)
    return (w_in.astype(BF16), w_dw_p, b_dw.reshape(1, C_CONV), ln_g.reshape(1, C_CONV),
            ln_b.reshape(1, C_CONV), w_pw.astype(BF16), w_out.astype(BF16))


def _encoder(x, layers, norm_g, attn_sink, final_norm_g, tabs):
    b, seq, _ = x.shape
    x2 = x.reshape(b * seq, D_MODEL)
    depth = len(layers)
    for l, (w_b, w_dw_p, b_dw, ln_g, ln_b, w_pw, w_out) in enumerate(layers):
        g = norm_g[l].reshape(1, D_MODEL)
        h, xn = _glu_proj(x2, g, w_b, seq)
        sga, q, kv, conv_out = _proj_conv(xn, w_b, tabs, h, w_dw_p, b_dw, ln_g, ln_b, w_pw, seq)
        x2 = _attention_outproj(q, kv, sga, conv_out, attn_sink[l], w_out, x2,
                                final_norm_g.reshape(1, D_MODEL), seq, final_norm=(l == depth - 1))
    return x2.reshape(b, seq, D_MODEL)


def kernel(x_prompt, x_sample, norm_g, w_in, w_dw, b_dw, conv_ln_g, conv_ln_b, w_pw, attn_sink, w_out,
           final_norm_g):
    depth = w_in.shape[0]
    layers = [_prep_layer(w_in[l], w_dw[l], b_dw[l], conv_ln_g[l], conv_ln_b[l], w_pw[l], w_out[l])
              for l in range(depth)]
    tabs = _rope_tables(max(x_prompt.shape[1], x_sample.shape[1]))
    outs = []
    for x in (x_prompt, x_sample):
        outs.append(_encoder(x, layers, norm_g, attn_sink, final_norm_g, tabs))
    return tuple(outs)
```

```python
import functools

import jax
import jax.numpy as jnp
from jax import lax
from jax.experimental import pallas as pl
from jax.experimental.pallas import tpu as pltpu

D_MODEL = 2048
C_CONV = 1024
CONV_WIDTH = 31
CONV_PAD = CONV_WIDTH // 2
HEAD_DIM = 128
N_HEADS = 8
N_KV_HEADS = 2
GROUP = N_HEADS // N_KV_HEADS
BLOCK = 128
ROPE_THETA = 10000.0
NORM_EPS = 1e-6
LN_EPS = 1e-5
D_ATTN = N_HEADS * HEAD_DIM
D_KV = N_KV_HEADS * HEAD_DIM
D_IN_PROJ = 3 * C_CONV + 2 * D_ATTN + 2 * D_KV
MASK_VALUE = -1e30
LOG2_E = 1.4426950408889634

F32 = jnp.float32
BF16 = jnp.bfloat16

LANES = 128
N_CB = C_CONV // LANES
HALO_ROWS = 16
TM_GLU = 1024
TM_PROJ = 512
W_TAIL_COLS = 512
GLU_COLS = 256
CONV_ROWS = 32
T_ATTN = 512
VMEM_LIMIT = 58 * 1024 * 1024

COL_VAL = 0
COL_GLU = C_CONV
COL_CGATE = 2 * C_CONV
COL_Q = 3 * C_CONV
COL_KV = COL_Q + D_ATTN
COL_AGATE = COL_KV + 2 * D_KV

_ARB = pltpu.ARBITRARY


def _silu(x):
    return x * jax.nn.sigmoid(x)


def _resident(shape, index=None):
    index = (0,) * len(shape) if index is None else index
    return pl.BlockSpec(shape, lambda i: index, pipeline_mode=pl.Buffered(1))


def _glu_kernel(x_ref, g_ref, w_ref, h_ref, xn_ref, xb_ref):
    x = x_ref[...]
    rs = lax.rsqrt(jnp.mean(x * x, axis=-1, keepdims=True) + NORM_EPS)
    xn = (x * rs * g_ref[...]).astype(BF16)
    xn_ref[...] = xn
    xb_ref[...] = xn
    for c in range(C_CONV // GLU_COLS):
        val = jnp.dot(xb_ref[...], w_ref[:, COL_VAL + c * GLU_COLS:COL_VAL + (c + 1) * GLU_COLS],
                      preferred_element_type=F32)
        glu = jnp.dot(xb_ref[...], w_ref[:, COL_GLU + c * GLU_COLS:COL_GLU + (c + 1) * GLU_COLS],
                      preferred_element_type=F32)
        h = val * jax.nn.sigmoid(glu)
        for s in range(GLU_COLS // LANES):
            h_ref[c * (GLU_COLS // LANES) + s, :, :] = h[:, s * LANES:(s + 1) * LANES]


def _glu_proj(x2, g, w, seq):
    m = x2.shape[0]
    tm = min(TM_GLU, seq)
    return pl.pallas_call(
        _glu_kernel,
        out_shape=(jax.ShapeDtypeStruct((N_CB, m, LANES), F32), jax.ShapeDtypeStruct((m, D_MODEL), BF16)),
        grid=(m // tm,),
        in_specs=[pl.BlockSpec((tm, D_MODEL), lambda i: (i, 0)),
                  _resident((1, D_MODEL)),
                  _resident((D_MODEL, 2 * C_CONV))],
        out_specs=(pl.BlockSpec((N_CB, tm, LANES), lambda i: (0, i, 0)),
                   pl.BlockSpec((tm, D_MODEL), lambda i: (i, 0))),
        scratch_shapes=[pltpu.VMEM((tm, D_MODEL), BF16)],
        compiler_params=pltpu.CompilerParams(
            dimension_semantics=(_ARB,), vmem_limit_bytes=VMEM_LIMIT),
        name="glu_proj",
    )(x2, g, w)


def _chain_zero(v):
    return jnp.minimum(jnp.abs(jnp.max(v, axis=1, keepdims=True)), 0.0)


def _proj_conv_kernel(xn_ref, wa_ref, wb_ref, wc_ref, wd_ref, cos_ref, sin_ref, h_ref, hp_ref, hn_ref,
                      wdw_ref, bdw_ref, lng_ref, lnb_ref, wpw_ref,
                      sga_ref, q_ref, kv_ref, co_ref,
                      xb_ref, abuf, hbuf, ybuf, sgc_ref, *, tm, tiles_per_seq):
    i = pl.program_id(0)
    t_in_seq = i % tiles_per_seq
    at_start = t_in_seq == 0
    at_end = t_in_seq == tiles_per_seq - 1
    hr = HALO_ROWS

    for cb in range(N_CB):
        hbuf[cb, 0:hr, :] = jnp.where(at_start, 0.0, hp_ref[cb])
        hbuf[cb, hr:hr + tm, :] = h_ref[cb]
        hbuf[cb, hr + tm:tm + 2 * hr, :] = jnp.where(at_end, 0.0, hn_ref[cb])

    rb = CONV_ROWS
    shift = hr - CONV_PAD
    dep = jnp.zeros((8, LANES), F32)
    for cb in range(N_CB):
        lanes = slice(cb * LANES, (cb + 1) * LANES)
        for r in range(tm // (2 * rb)):
            r0 = r * 2 * rb
            acc_e = jnp.broadcast_to(bdw_ref[:, lanes], (rb, LANES))
            acc_o = acc_e
            for k in range(CONV_WIDTH):
                w = jnp.broadcast_to(wdw_ref[k:k + 1, lanes], (8, LANES)) + dep
                w = jnp.concatenate([w] * (rb // 8), axis=0)
                acc_e = acc_e + w * hbuf[cb, pl.ds(r0 + k + shift, rb, stride=2), :]
                acc_o = acc_o + w * hbuf[cb, pl.ds(r0 + k + shift + 1, rb, stride=2), :]
            ybuf[cb, pl.ds(r0, rb, stride=2), :] = acc_e
            ybuf[cb, pl.ds(r0 + 1, rb, stride=2), :] = acc_o
            dep = jnp.broadcast_to(_chain_zero(acc_e[0:8, :] + acc_o[0:8, :]), (8, LANES))

    xb_ref[...] = xn_ref[...]

    def proj(c0, n):
        for ref, base, width in ((wa_ref, COL_CGATE, COL_KV - COL_CGATE), (wb_ref, COL_KV, W_TAIL_COLS),
                                 (wc_ref, COL_AGATE, W_TAIL_COLS), (wd_ref, COL_AGATE + W_TAIL_COLS, W_TAIL_COLS)):
            if base <= c0 and c0 + n <= base + width:
                return jnp.dot(xb_ref[...], ref[:, c0 - base:c0 - base + n], preferred_element_type=F32)
        raise ValueError((c0, n))

    def rope(t):
        return t * cos_ref[...] + pltpu.roll(t, HEAD_DIM // 2, 1) * sin_ref[...]

    half = C_CONV // 2
    for c in range(D_ATTN // half):
        sga_ref[:, c * half:(c + 1) * half] = _silu(proj(COL_AGATE + c * half, half))

    scale = HEAD_DIM ** -0.5 * LOG2_E
    for c in range(D_ATTN // half):
        z = proj(COL_Q + c * half, half)
        for hh in range(half // HEAD_DIM):
            cols = slice(hh * HEAD_DIM, (hh + 1) * HEAD_DIM)
            q_ref[:, c * half + hh * HEAD_DIM:c * half + (hh + 1) * HEAD_DIM] = (
                (rope(z[:, cols]) * scale).astype(BF16))

    z = proj(COL_KV, 2 * D_KV)
    for hh in range(N_KV_HEADS):
        cols = slice(hh * HEAD_DIM, (hh + 1) * HEAD_DIM)
        kv_ref[:, cols] = rope(z[:, cols]).astype(BF16)
    kv_ref[:, D_KV:2 * D_KV] = z[:, D_KV:2 * D_KV].astype(BF16)

    for c in range(C_CONV // half):
        sgc_ref[:, c * half:(c + 1) * half] = _silu(proj(COL_CGATE + c * half, half))

    y = jnp.concatenate([ybuf[cb] for cb in range(N_CB)], axis=1)
    mu = jnp.mean(y, axis=-1, keepdims=True)
    yc = y - mu
    var = jnp.mean(yc * yc, axis=-1, keepdims=True)
    abuf[...] = _silu(yc * lax.rsqrt(var + LN_EPS) * lng_ref[...] + lnb_ref[...]).astype(BF16)
    o = jnp.dot(abuf[...], wpw_ref[...], preferred_element_type=F32)
    co_ref[...] = (o * sgc_ref[...]).astype(BF16)


def _proj_conv(xn, w, tabs, h, w_dw, b_dw, ln_g, ln_b, w_pw, seq):
    m = xn.shape[0]
    tm = min(TM_PROJ, seq)
    tiles_per_seq = seq // tm
    hb = tm // HALO_ROWS
    n_hblocks = m // HALO_ROWS
    tab_spec = pl.BlockSpec((tm, HEAD_DIM), lambda i: (i % tiles_per_seq, 0))
    return pl.pallas_call(
        functools.partial(_proj_conv_kernel, tm=tm, tiles_per_seq=tiles_per_seq),
        out_shape=(jax.ShapeDtypeStruct((m, D_ATTN), F32),
                   jax.ShapeDtypeStruct((m, D_ATTN), BF16),
                   jax.ShapeDtypeStruct((m, 2 * D_KV), BF16),
                   jax.ShapeDtypeStruct((m, C_CONV), BF16)),
        grid=(m // tm,),
        in_specs=[pl.BlockSpec((tm, D_MODEL), lambda i: (i, 0)),
                  _resident((D_MODEL, COL_KV - COL_CGATE), (0, COL_CGATE // (COL_KV - COL_CGATE))),
                  _resident((D_MODEL, W_TAIL_COLS), (0, COL_KV // W_TAIL_COLS)),
                  _resident((D_MODEL, W_TAIL_COLS), (0, COL_AGATE // W_TAIL_COLS)),
                  _resident((D_MODEL, W_TAIL_COLS), (0, COL_AGATE // W_TAIL_COLS + 1)),
                  tab_spec, tab_spec,
                  pl.BlockSpec((N_CB, tm, LANES), lambda i: (0, i, 0)),
                  pl.BlockSpec((N_CB, HALO_ROWS, LANES), lambda i: (0, jnp.maximum(i * hb - 1, 0), 0)),
                  pl.BlockSpec((N_CB, HALO_ROWS, LANES),
                               lambda i: (0, jnp.minimum((i + 1) * hb, n_hblocks - 1), 0)),
                  _resident((CONV_WIDTH + 1, C_CONV)), _resident((1, C_CONV)), _resident((1, C_CONV)),
                  _resident((1, C_CONV)), _resident((C_CONV, C_CONV))],
        out_specs=(pl.BlockSpec((tm, D_ATTN), lambda i: (i, 0)),
                   pl.BlockSpec((tm, D_ATTN), lambda i: (i, 0)),
                   pl.BlockSpec((tm, 2 * D_KV), lambda i: (i, 0)),
                   pl.BlockSpec((tm, C_CONV), lambda i: (i, 0))),
        scratch_shapes=[pltpu.VMEM((tm, D_MODEL), BF16),
                        pltpu.VMEM((tm, C_CONV), BF16),
                        pltpu.VMEM((N_CB, tm + 2 * HALO_ROWS, LANES), F32),
                        pltpu.VMEM((N_CB, tm, LANES), F32),
                        pltpu.VMEM((tm, C_CONV), F32)],
        compiler_params=pltpu.CompilerParams(
            dimension_semantics=(_ARB,), vmem_limit_bytes=VMEM_LIMIT),
        name="proj_conv",
    )(xn, w, w, w, w, *tabs, h, h, h, w_dw, b_dw, ln_g, ln_b, w_pw)


def _attn_out_kernel(sink_ref, q_ref, kv_ref, kvp_ref, kvn_ref, sg_ref, c_ref, w_ref, x_ref, g_ref,
                     out_ref, kvbuf, *, t, tiles_per_seq, final_norm):
    i = pl.program_id(0)
    t_in_seq = i % tiles_per_seq
    off_start = jnp.where(t_in_seq == 0, 2 * BLOCK, 0)
    off_end = jnp.where(t_in_seq == tiles_per_seq - 1, 2 * BLOCK, 0)

    kvbuf[0:BLOCK, :] = kvp_ref[...]
    kvbuf[BLOCK:BLOCK + t, :] = kv_ref[...]
    kvbuf[BLOCK + t:2 * BLOCK + t, :] = kvn_ref[...]

    qi = lax.broadcasted_iota(jnp.int32, (BLOCK, BLOCK), 0)
    kj = lax.broadcasted_iota(jnp.int32, (BLOCK, BLOCK), 1)
    nqb = t // BLOCK

    rows_out = []
    for qb in range(nqb):
        mask_prev = (kj >= qi + off_start) if qb == 0 else (kj >= qi)
        mask_next = (kj + off_end <= qi) if qb == nqb - 1 else (kj <= qi)
        qrows = slice(qb * BLOCK, (qb + 1) * BLOCK)
        krows = slice(qb * BLOCK, (qb + 3) * BLOCK)
        heads_out = []
        for h2 in range(N_KV_HEADS):
            kk = kvbuf[krows, h2 * HEAD_DIM:(h2 + 1) * HEAD_DIM]
            vv = kvbuf[krows, D_KV + h2 * HEAD_DIM:D_KV + (h2 + 1) * HEAD_DIM]
            qs = jnp.concatenate(
                [q_ref[qrows, (h2 * GROUP + g) * HEAD_DIM:(h2 * GROUP + g + 1) * HEAD_DIM]
                 for g in range(GROUP)], axis=0)
            s = lax.dot_general(qs, kk, (((1,), (1,)), ((), ())), preferred_element_type=F32)
            ps, rinv = [], []
            for g in range(GROUP):
                sink = sink_ref[h2 * GROUP + g] * LOG2_E
                sg = s[g * BLOCK:(g + 1) * BLOCK, :]
                s0 = jnp.where(mask_prev, sg[:, 0:BLOCK], MASK_VALUE)
                s1 = sg[:, BLOCK:2 * BLOCK]
                s2 = jnp.where(mask_next, sg[:, 2 * BLOCK:3 * BLOCK], MASK_VALUE)
                mx = jnp.max(jnp.maximum(jnp.maximum(s0, s1), s2), axis=-1, keepdims=True)
                mx = jnp.maximum(mx, sink)
                p0, p1, p2 = jnp.exp2(s0 - mx), jnp.exp2(s1 - mx), jnp.exp2(s2 - mx)
                denom = jnp.sum(p0 + p1 + p2, axis=-1, keepdims=True) + jnp.exp2(sink - mx)
                rinv.append(1.0 / denom)
                ps.append(jnp.concatenate([p0, p1, p2], axis=1).astype(BF16))
            o = jnp.dot(jnp.concatenate(ps, axis=0), vv, preferred_element_type=F32)
            for g in range(GROUP):
                hcols = slice((h2 * GROUP + g) * HEAD_DIM, (h2 * GROUP + g + 1) * HEAD_DIM)
                og = o[g * BLOCK:(g + 1) * BLOCK, :] * rinv[g]
                heads_out.append((og * sg_ref[qrows, hcols]).astype(BF16))
        rows_out.append(jnp.concatenate(heads_out, axis=1))
    attn = jnp.concatenate(rows_out, axis=0)

    y = (x_ref[...]
         + jnp.dot(c_ref[...], w_ref[0:C_CONV, :], preferred_element_type=F32)
         + jnp.dot(attn, w_ref[C_CONV:C_CONV + D_ATTN, :], preferred_element_type=F32))
    if final_norm:
        y = y * lax.rsqrt(jnp.mean(y * y, axis=-1, keepdims=True) + NORM_EPS) * g_ref[...]
    out_ref[...] = y


def _attention_outproj(q, kv, sg, conv_out, sink, w_out, x2, g, seq, final_norm):
    m = x2.shape[0]
    t = min(T_ATTN, seq)
    tiles_per_seq = seq // t
    bpt = t // BLOCK
    n_blocks = m // BLOCK
    return pl.pallas_call(
        functools.partial(_attn_out_kernel, t=t, tiles_per_seq=tiles_per_seq, final_norm=final_norm),
        out_shape=jax.ShapeDtypeStruct((m, D_MODEL), F32),
        grid=(m // t,),
        in_specs=[pl.BlockSpec(memory_space=pltpu.SMEM),
                  pl.BlockSpec((t, D_ATTN), lambda i: (i, 0)),
                  pl.BlockSpec((t, 2 * D_KV), lambda i: (i, 0)),
                  pl.BlockSpec((BLOCK, 2 * D_KV), lambda i: (jnp.maximum(i * bpt - 1, 0), 0)),
                  pl.BlockSpec((BLOCK, 2 * D_KV), lambda i: (jnp.minimum((i + 1) * bpt, n_blocks - 1), 0)),
                  pl.BlockSpec((t, D_ATTN), lambda i: (i, 0)),
                  pl.BlockSpec((t, C_CONV), lambda i: (i, 0)),
                  _resident((C_CONV + D_ATTN, D_MODEL)),
                  pl.BlockSpec((t, D_MODEL), lambda i: (i, 0)),
                  _resident((1, D_MODEL))],
        out_specs=pl.BlockSpec((t, D_MODEL), lambda i: (i, 0)),
        scratch_shapes=[pltpu.VMEM((t + 2 * BLOCK, 2 * D_KV), BF16)],
        compiler_params=pltpu.CompilerParams(
            dimension_semantics=(_ARB,), vmem_limit_bytes=VMEM_LIMIT),
        name="attn_outproj",
    )(sink, q, kv, kv, kv, sg, conv_out, w_out, x2, g)


def _rope_tables(seq):
    half = HEAD_DIM // 2
    inv_freq = 1.0 / (ROPE_THETA ** (jnp.arange(half, dtype=F32) / half))
    ang = jnp.arange(seq).astype(F32)[:, None] * inv_freq[None, :]
    cos, sin = jnp.cos(ang), jnp.sin(ang)
    cos_full = jnp.concatenate([cos, cos], axis=-1)
    sin_signed = jnp.concatenate([-sin, sin], axis=-1)
    return (cos_full, sin_signed)


def _prep_layer(w_in, w_dw, b_dw, ln_g, ln_b, w_pw, w_out):
    w_dw_p = jnp.concatenate([w_dw, jnp.zeros((1, C_CONV), F32)], axis=0)
    return (w_in.astype(BF16), w_dw_p, b_dw.reshape(1, C_CONV), ln_g.reshape(1, C_CONV),
            ln_b.reshape(1, C_CONV), w_pw.astype(BF16), w_out.astype(BF16))


def _encoder(x, layers, norm_g, attn_sink, final_norm_g, tabs):
    b, seq, _ = x.shape
    x2 = x.reshape(b * seq, D_MODEL)
    depth = len(layers)
    for l, (w_b, w_dw_p, b_dw, ln_g, ln_b, w_pw, w_out) in enumerate(layers):
        g = norm_g[l].reshape(1, D_MODEL)
        h, xn = _glu_proj(x2, g, w_b, seq)
        sga, q, kv, conv_out = _proj_conv(xn, w_b, tabs, h, w_dw_p, b_dw, ln_g, ln_b, w_pw, seq)
        x2 = _attention_outproj(q, kv, sga, conv_out, attn_sink[l], w_out, x2,
                                final_norm_g.reshape(1, D_MODEL), seq, final_norm=(l == depth - 1))
    return x2.reshape(b, seq, D_MODEL)


def kernel(x_prompt, x_sample, norm_g, w_in, w_dw, b_dw, conv_ln_g, conv_ln_b, w_pw, attn_sink, w_out,
           final_norm_g):
    depth = w_in.shape[0]
    layers = [_prep_layer(w_in[l], w_dw[l], b_dw[l], conv_ln_g[l], conv_ln_b[l], w_pw[l], w_out[l])
              for l in range(depth)]
    tabs = _rope_tables(max(x_prompt.shape[1], x_sample.shape[1]))
    outs = []
    for x in (x_prompt, x_sample):
        outs.append(_encoder(x, layers, norm_g, attn_sink, final_norm_g, tabs))
    return tuple(outs)
```

```python
import functools

import jax
import jax.numpy as jnp
from jax import lax
from jax.experimental import pallas as pl
from jax.experimental.pallas import tpu as pltpu

D_MODEL = 2048
C_CONV = 1024
CONV_WIDTH = 31
CONV_PAD = CONV_WIDTH // 2
HEAD_DIM = 128
N_HEADS = 8
N_KV_HEADS = 2
GROUP = N_HEADS // N_KV_HEADS
BLOCK = 128
ROPE_THETA = 10000.0
NORM_EPS = 1e-6
LN_EPS = 1e-5
D_ATTN = N_HEADS * HEAD_DIM
D_KV = N_KV_HEADS * HEAD_DIM
D_IN_PROJ = 3 * C_CONV + 2 * D_ATTN + 2 * D_KV
MASK_VALUE = -1e30
LOG2_E = 1.4426950408889634

F32 = jnp.float32
BF16 = jnp.bfloat16

LANES = 128
N_CB = C_CONV // LANES
HALO_ROWS = 16
TM_GLU = 1024
TM_PROJ = 512
W_TAIL_COLS = 512
GLU_COLS = 256
CONV_ROWS = 32
T_ATTN = 512
VMEM_LIMIT = 58 * 1024 * 1024
ROPE_SPLIT = 64

COL_VAL = 0
COL_GLU = C_CONV
COL_CGATE = 2 * C_CONV
COL_Q = 3 * C_CONV
COL_KV = COL_Q + D_ATTN
COL_AGATE = COL_KV + 2 * D_KV

_ARB = pltpu.ARBITRARY


def _silu(x):
    return x * jax.nn.sigmoid(x)


def _resident(shape, index=None):
    index = (0,) * len(shape) if index is None else index
    return pl.BlockSpec(shape, lambda i: index, pipeline_mode=pl.Buffered(1))


def _glu_kernel(x_ref, g_ref, w_ref, h_ref, xn_ref, xb_ref):
    x = x_ref[...]
    rs = lax.rsqrt(jnp.mean(x * x, axis=-1, keepdims=True) + NORM_EPS)
    xn = (x * rs * g_ref[...]).astype(BF16)
    xn_ref[...] = xn
    xb_ref[...] = xn
    for c in range(C_CONV // GLU_COLS):
        val = jnp.dot(xb_ref[...], w_ref[:, COL_VAL + c * GLU_COLS:COL_VAL + (c + 1) * GLU_COLS],
                      preferred_element_type=F32)
        glu = jnp.dot(xb_ref[...], w_ref[:, COL_GLU + c * GLU_COLS:COL_GLU + (c + 1) * GLU_COLS],
                      preferred_element_type=F32)
        h = val * jax.nn.sigmoid(glu)
        for s in range(GLU_COLS // LANES):
            h_ref[c * (GLU_COLS // LANES) + s, :, :] = h[:, s * LANES:(s + 1) * LANES]


def _glu_proj(x2, g, w, seq):
    m = x2.shape[0]
    tm = min(TM_GLU, seq)
    return pl.pallas_call(
        _glu_kernel,
        out_shape=(jax.ShapeDtypeStruct((N_CB, m, LANES), F32), jax.ShapeDtypeStruct((m, D_MODEL), BF16)),
        grid=(m // tm,),
        in_specs=[pl.BlockSpec((tm, D_MODEL), lambda i: (i, 0)),
                  _resident((1, D_MODEL)),
                  _resident((D_MODEL, 2 * C_CONV))],
        out_specs=(pl.BlockSpec((N_CB, tm, LANES), lambda i: (0, i, 0)),
                   pl.BlockSpec((tm, D_MODEL), lambda i: (i, 0))),
        scratch_shapes=[pltpu.VMEM((tm, D_MODEL), BF16)],
        compiler_params=pltpu.CompilerParams(
            dimension_semantics=(_ARB,), vmem_limit_bytes=VMEM_LIMIT),
        name="glu_proj",
    )(x2, g, w)


def _chain_zero(v):
    return jnp.minimum(jnp.abs(jnp.max(v, axis=1, keepdims=True)), 0.0)


def _proj_conv_kernel(xn_ref, wa_ref, wb_ref, wc_ref, wd_ref, cos_ref, sin_ref, h_ref, hp_ref, hn_ref,
                      wdw_ref, bdw_ref, lng_ref, lnb_ref, wpw_ref,
                      sga_ref, q_ref, kv_ref, co_ref,
                      xb_ref, abuf, hbuf, ybuf, sgc_ref, *, tm, tiles_per_seq):
    i = pl.program_id(0)
    t_in_seq = i % tiles_per_seq
    at_start = t_in_seq == 0
    at_end = t_in_seq == tiles_per_seq - 1
    hr = HALO_ROWS

    for cb in range(N_CB):
        hbuf[cb, 0:hr, :] = jnp.where(at_start, 0.0, hp_ref[cb])
        hbuf[cb, hr:hr + tm, :] = h_ref[cb]
        hbuf[cb, hr + tm:tm + 2 * hr, :] = jnp.where(at_end, 0.0, hn_ref[cb])

    rb = CONV_ROWS
    shift = hr - CONV_PAD
    dep = jnp.zeros((8, LANES), F32)
    for cb in range(N_CB):
        lanes = slice(cb * LANES, (cb + 1) * LANES)
        for r in range(tm // (2 * rb)):
            r0 = r * 2 * rb
            acc_e = jnp.broadcast_to(bdw_ref[:, lanes], (rb, LANES))
            acc_o = acc_e
            for k in range(CONV_WIDTH):
                w = jnp.broadcast_to(wdw_ref[k:k + 1, lanes], (8, LANES)) + dep
                w = jnp.concatenate([w] * (rb // 8), axis=0)
                acc_e = acc_e + w * hbuf[cb, pl.ds(r0 + k + shift, rb, stride=2), :]
                acc_o = acc_o + w * hbuf[cb, pl.ds(r0 + k + shift + 1, rb, stride=2), :]
            ybuf[cb, pl.ds(r0, rb, stride=2), :] = acc_e
            ybuf[cb, pl.ds(r0 + 1, rb, stride=2), :] = acc_o
            dep = jnp.broadcast_to(_chain_zero(acc_e[0:8, :] + acc_o[0:8, :]), (8, LANES))

    xb_ref[...] = xn_ref[...]

    def proj(c0, n):
        for ref, base, width in ((wa_ref, COL_CGATE, COL_KV - COL_CGATE), (wb_ref, COL_KV, W_TAIL_COLS),
                                 (wc_ref, COL_AGATE, W_TAIL_COLS), (wd_ref, COL_AGATE + W_TAIL_COLS, W_TAIL_COLS)):
            if base <= c0 and c0 + n <= base + width:
                return jnp.dot(xb_ref[...], ref[:, c0 - base:c0 - base + n], preferred_element_type=F32)
        raise ValueError((c0, n))

    def rope(t):
        return t * cos_ref[...] + pltpu.roll(t, HEAD_DIM // 2, 1) * sin_ref[...]

    half = C_CONV // 2
    for c in range(D_ATTN // half):
        sga_ref[:, c * half:(c + 1) * half] = _silu(proj(COL_AGATE + c * half, half))

    scale = HEAD_DIM ** -0.5 * LOG2_E
    for c in range(D_ATTN // half):
        z = proj(COL_Q + c * half, half)
        for hh in range(half // HEAD_DIM):
            cols = slice(hh * HEAD_DIM, (hh + 1) * HEAD_DIM)
            q_ref[:, c * half + hh * HEAD_DIM:c * half + (hh + 1) * HEAD_DIM] = (
                (rope(z[:, cols]) * scale).astype(BF16))

    z = proj(COL_KV, 2 * D_KV)
    for hh in range(N_KV_HEADS):
        cols = slice(hh * HEAD_DIM, (hh + 1) * HEAD_DIM)
        kv_ref[:, cols] = rope(z[:, cols]).astype(BF16)
    kv_ref[:, D_KV:2 * D_KV] = z[:, D_KV:2 * D_KV].astype(BF16)

    for c in range(C_CONV // half):
        sgc_ref[:, c * half:(c + 1) * half] = _silu(proj(COL_CGATE + c * half, half))

    y = jnp.concatenate([ybuf[cb] for cb in range(N_CB)], axis=1)
    mu = jnp.mean(y, axis=-1, keepdims=True)
    yc = y - mu
    var = jnp.mean(yc * yc, axis=-1, keepdims=True)
    abuf[...] = _silu(yc * lax.rsqrt(var + LN_EPS) * lng_ref[...] + lnb_ref[...]).astype(BF16)
    o = jnp.dot(abuf[...], wpw_ref[...], preferred_element_type=F32)
    co_ref[...] = (o * sgc_ref[...]).astype(BF16)


def _proj_conv(xn, w, tabs, h, w_dw, b_dw, ln_g, ln_b, w_pw, seq):
    m = xn.shape[0]
    tm = min(TM_PROJ, seq)
    tiles_per_seq = seq // tm
    hb = tm // HALO_ROWS
    n_hblocks = m // HALO_ROWS
    tab_spec = pl.BlockSpec((tm, HEAD_DIM), lambda i: (i % tiles_per_seq, 0))
    return pl.pallas_call(
        functools.partial(_proj_conv_kernel, tm=tm, tiles_per_seq=tiles_per_seq),
        out_shape=(jax.ShapeDtypeStruct((m, D_ATTN), F32),
                   jax.ShapeDtypeStruct((m, D_ATTN), BF16),
                   jax.ShapeDtypeStruct((m, 2 * D_KV), BF16),
                   jax.ShapeDtypeStruct((m, C_CONV), BF16)),
        grid=(m // tm,),
        in_specs=[pl.BlockSpec((tm, D_MODEL), lambda i: (i, 0)),
                  _resident((D_MODEL, COL_KV - COL_CGATE), (0, COL_CGATE // (COL_KV - COL_CGATE))),
                  _resident((D_MODEL, W_TAIL_COLS), (0, COL_KV // W_TAIL_COLS)),
                  _resident((D_MODEL, W_TAIL_COLS), (0, COL_AGATE // W_TAIL_COLS)),
                  _resident((D_MODEL, W_TAIL_COLS), (0, COL_AGATE // W_TAIL_COLS + 1)),
                  tab_spec, tab_spec,
                  pl.BlockSpec((N_CB, tm, LANES), lambda i: (0, i, 0)),
                  pl.BlockSpec((N_CB, HALO_ROWS, LANES), lambda i: (0, jnp.maximum(i * hb - 1, 0), 0)),
                  pl.BlockSpec((N_CB, HALO_ROWS, LANES),
                               lambda i: (0, jnp.minimum((i + 1) * hb, n_hblocks - 1), 0)),
                  _resident((CONV_WIDTH, C_CONV)), _resident((1, C_CONV)), _resident((1, C_CONV)),
                  _resident((1, C_CONV)), _resident((C_CONV, C_CONV))],
        out_specs=(pl.BlockSpec((tm, D_ATTN), lambda i: (i, 0)),
                   pl.BlockSpec((tm, D_ATTN), lambda i: (i, 0)),
                   pl.BlockSpec((tm, 2 * D_KV), lambda i: (i, 0)),
                   pl.BlockSpec((tm, C_CONV), lambda i: (i, 0))),
        scratch_shapes=[pltpu.VMEM((tm, D_MODEL), BF16),
                        pltpu.VMEM((tm, C_CONV), BF16),
                        pltpu.VMEM((N_CB, tm + 2 * HALO_ROWS, LANES), F32),
                        pltpu.VMEM((N_CB, tm, LANES), F32),
                        pltpu.VMEM((tm, C_CONV), F32)],
        compiler_params=pltpu.CompilerParams(
            dimension_semantics=(_ARB,), vmem_limit_bytes=VMEM_LIMIT),
        name="proj_conv",
    )(xn, w, w, w, w, *tabs, h, h, h, w_dw, b_dw, ln_g, ln_b, w_pw)


def _attn_out_kernel(sink_ref, q_ref, kv_ref, kvp_ref, kvn_ref, sg_ref, c_ref, w_ref, x_ref, g_ref,
                     out_ref, kvbuf, *, t, tiles_per_seq, final_norm):
    i = pl.program_id(0)
    t_in_seq = i % tiles_per_seq
    off_start = jnp.where(t_in_seq == 0, 2 * BLOCK, 0)
    off_end = jnp.where(t_in_seq == tiles_per_seq - 1, 2 * BLOCK, 0)

    kvbuf[0:BLOCK, :] = kvp_ref[...]
    kvbuf[BLOCK:BLOCK + t, :] = kv_ref[...]
    kvbuf[BLOCK + t:2 * BLOCK + t, :] = kvn_ref[...]

    qi = lax.broadcasted_iota(jnp.int32, (BLOCK, BLOCK), 0)
    kj = lax.broadcasted_iota(jnp.int32, (BLOCK, BLOCK), 1)
    nqb = t // BLOCK

    rows_out = []
    for qb in range(nqb):
        mask_prev = (kj >= qi + off_start) if qb == 0 else (kj >= qi)
        mask_next = (kj + off_end <= qi) if qb == nqb - 1 else (kj <= qi)
        qrows = slice(qb * BLOCK, (qb + 1) * BLOCK)
        krows = slice(qb * BLOCK, (qb + 3) * BLOCK)
        heads_out = []
        for h2 in range(N_KV_HEADS):
            kk = kvbuf[krows, h2 * HEAD_DIM:(h2 + 1) * HEAD_DIM]
            vv = kvbuf[krows, D_KV + h2 * HEAD_DIM:D_KV + (h2 + 1) * HEAD_DIM]
            qs = jnp.concatenate(
                [q_ref[qrows, (h2 * GROUP + g) * HEAD_DIM:(h2 * GROUP + g + 1) * HEAD_DIM]
                 for g in range(GROUP)], axis=0)
            s = lax.dot_general(qs, kk, (((1,), (1,)), ((), ())), preferred_element_type=F32)
            ps, rinv = [], []
            for g in range(GROUP):
                sink = sink_ref[h2 * GROUP + g] * LOG2_E
                sg = s[g * BLOCK:(g + 1) * BLOCK, :]
                s0 = jnp.where(mask_prev, sg[:, 0:BLOCK], MASK_VALUE)
                s1 = sg[:, BLOCK:2 * BLOCK]
                s2 = jnp.where(mask_next, sg[:, 2 * BLOCK:3 * BLOCK], MASK_VALUE)
                mx = jnp.max(jnp.maximum(jnp.maximum(s0, s1), s2), axis=-1, keepdims=True)
                mx = jnp.maximum(mx, sink)
                p0, p1, p2 = jnp.exp2(s0 - mx), jnp.exp2(s1 - mx), jnp.exp2(s2 - mx)
                denom = jnp.sum(p0 + p1 + p2, axis=-1, keepdims=True) + jnp.exp2(sink - mx)
                rinv.append(1.0 / denom)
                ps.append(jnp.concatenate([p0, p1, p2], axis=1).astype(BF16))
            o = jnp.dot(jnp.concatenate(ps, axis=0), vv, preferred_element_type=F32)
            for g in range(GROUP):
                hcols = slice((h2 * GROUP + g) * HEAD_DIM, (h2 * GROUP + g + 1) * HEAD_DIM)
                og = o[g * BLOCK:(g + 1) * BLOCK, :] * rinv[g]
                heads_out.append((og * sg_ref[qrows, hcols]).astype(BF16))
        rows_out.append(jnp.concatenate(heads_out, axis=1))
    attn = jnp.concatenate(rows_out, axis=0)

    y = (x_ref[...]
         + jnp.dot(c_ref[...], w_ref[0:C_CONV, :], preferred_element_type=F32)
         + jnp.dot(attn, w_ref[C_CONV:C_CONV + D_ATTN, :], preferred_element_type=F32))
    if final_norm:
        y = y * lax.rsqrt(jnp.mean(y * y, axis=-1, keepdims=True) + NORM_EPS) * g_ref[...]
    out_ref[...] = y


def _attention_outproj(q, kv, sg, conv_out, sink, w_out, x2, g, seq, final_norm):
    m = x2.shape[0]
    t = min(T_ATTN, seq)
    tiles_per_seq = seq // t
    bpt = t // BLOCK
    n_blocks = m // BLOCK
    return pl.pallas_call(
        functools.partial(_attn_out_kernel, t=t, tiles_per_seq=tiles_per_seq, final_norm=final_norm),
        out_shape=jax.ShapeDtypeStruct((m, D_MODEL), F32),
        grid=(m // t,),
        in_specs=[pl.BlockSpec(memory_space=pltpu.SMEM),
                  pl.BlockSpec((t, D_ATTN), lambda i: (i, 0)),
                  pl.BlockSpec((t, 2 * D_KV), lambda i: (i, 0)),
                  pl.BlockSpec((BLOCK, 2 * D_KV), lambda i: (jnp.maximum(i * bpt - 1, 0), 0)),
                  pl.BlockSpec((BLOCK, 2 * D_KV), lambda i: (jnp.minimum((i + 1) * bpt, n_blocks - 1), 0)),
                  pl.BlockSpec((t, D_ATTN), lambda i: (i, 0)),
                  pl.BlockSpec((t, C_CONV), lambda i: (i, 0)),
                  _resident((C_CONV + D_ATTN, D_MODEL)),
                  pl.BlockSpec((t, D_MODEL), lambda i: (i, 0)),
                  _resident((1, D_MODEL))],
        out_specs=pl.BlockSpec((t, D_MODEL), lambda i: (i, 0)),
        scratch_shapes=[pltpu.VMEM((t + 2 * BLOCK, 2 * D_KV), BF16)],
        compiler_params=pltpu.CompilerParams(
            dimension_semantics=(_ARB,), vmem_limit_bytes=VMEM_LIMIT),
        name="attn_outproj",
    )(sink, q, kv, kv, kv, sg, conv_out, w_out, x2, g)


def _rope_tables(seq):
    half = HEAD_DIM // 2
    inv_freq = 1.0 / (ROPE_THETA ** (jnp.arange(half, dtype=F32) / half))
    freq = jnp.concatenate([inv_freq, inv_freq])
    sign = jnp.concatenate([-jnp.ones((half,), F32), jnp.ones((half,), F32)])
    split = min(ROPE_SPLIT, seq)
    hi = (jnp.arange(seq // split) * split).astype(F32)[:, None] * freq[None, :]
    lo = jnp.arange(split).astype(F32)[:, None] * freq[None, :]
    ch, sh = jnp.cos(hi)[:, None, :], jnp.sin(hi)[:, None, :]
    cl, sl = jnp.cos(lo)[None, :, :], jnp.sin(lo)[None, :, :]
    cos_full = (ch * cl - sh * sl).reshape(seq, HEAD_DIM)
    sin_signed = ((sh * cl + ch * sl) * sign).reshape(seq, HEAD_DIM)
    return (cos_full, sin_signed)


def _prep_layer(w_in, w_dw, b_dw, ln_g, ln_b, w_pw, w_out):
    return (w_in.astype(BF16), w_dw, b_dw.reshape(1, C_CONV), ln_g.reshape(1, C_CONV),
            ln_b.reshape(1, C_CONV), w_pw.astype(BF16), w_out.astype(BF16))


def _encoder(x, layers, norm_g, attn_sink, final_norm_g, tabs):
    b, seq, _ = x.shape
    x2 = x.reshape(b * seq, D_MODEL)
    depth = len(layers)
    for l, (w_b, w_dw, b_dw, ln_g, ln_b, w_pw, w_out) in enumerate(layers):
        g = norm_g[l].reshape(1, D_MODEL)
        h, xn = _glu_proj(x2, g, w_b, seq)
        sga, q, kv, conv_out = _proj_conv(xn, w_b, tabs, h, w_dw, b_dw, ln_g, ln_b, w_pw, seq)
        x2 = _attention_outproj(q, kv, sga, conv_out, attn_sink[l], w_out, x2,
                                final_norm_g.reshape(1, D_MODEL), seq, final_norm=(l == depth - 1))
    return x2.reshape(b, seq, D_MODEL)


def kernel(x_prompt, x_sample, norm_g, w_in, w_dw, b_dw, conv_ln_g, conv_ln_b, w_pw, attn_sink, w_out,
           final_norm_g):
    depth = w_in.shape[0]
    layers = [_prep_layer(w_in[l], w_dw[l], b_dw[l], conv_ln_g[l], conv_ln_b[l], w_pw[l], w_out[l])
              for l in range(depth)]
    tabs = _rope_tables(max(x_prompt.shape[1], x_sample.shape[1]))
    outs = []
    for x in (x_prompt, x_sample):
        outs.append(_encoder(x, layers, norm_g, attn_sink, final_norm_g, tabs))
    return tuple(outs)
```
